```python
import math
import jax, jax.numpy as jnp
from jax import lax
import numpy as np

D_MODEL = 1024
BATCH = 8
SEQ = 4096
DEPTH = 1

CHUNK = 64
PLE_DIM = 256
MIX_WIDTH = D_MODEL
SSM_WIDTH = MIX_WIDTH // 2
POOL_WIDTH = MIX_WIDTH - SSM_WIDTH
SSM_GROUP = 16
SSM_GROUPS = SSM_WIDTH // SSM_GROUP
SSM_STATE = 64
POOL_WINDOWS = (2, 4, 8, 16)
POOL_GROUPS = len(POOL_WINDOWS)
POOL_GROUP_WIDTH = POOL_WIDTH // POOL_GROUPS
POOL_PAD = max(POOL_WINDOWS)
D_FF = 2816
FFN_RES = 0.5
EPS = 1e-6
DT_MIN = 1e-3
DT_MAX = 1e-1

kernel_name = 'hymba_s5_pool_macaron_block'


def rmsnorm(x, g):
    xf = x.astype(jnp.float32)
    y = xf * lax.rsqrt(jnp.mean(xf * xf, axis=-1, keepdims=True) + EPS)
    return (y * g.astype(jnp.float32)).astype(x.dtype)


def swiglu(x, w1, w3, w2):
    return (jax.nn.silu(x @ w1) * (x @ w3)) @ w2


def _complex_linear_combine(e1, e2):
    a1r, a1i, b1r, b1i = e1
    a2r, a2i, b2r, b2i = e2
    return (a2r * a1r - a2i * a1i,
            a2r * a1i + a2i * a1r,
            a2r * b1r - a2i * b1i + b2r,
            a2r * b1i + a2i * b1r + b2i)


def s5_mixer(u, a_re, a_im, log_dt, b_re, b_im, c_re, c_im, d_skip, w_glu, b_glu):
    f32 = jnp.float32
    bsz, seqlen, _ = u.shape
    uf = u.astype(f32)
    ug = uf.reshape(bsz, seqlen, SSM_GROUPS, SSM_GROUP)
    dt = jnp.exp(log_dt.astype(f32))[:, None]
    lam_r = a_re.astype(f32)
    lam_i = a_im.astype(f32)
    mag = jnp.exp(lam_r * dt)
    abar_r = mag * jnp.cos(lam_i * dt)
    abar_i = mag * jnp.sin(lam_i * dt)
    den = lam_r * lam_r + lam_i * lam_i
    nr = abar_r - 1.0
    f_r = (nr * lam_r + abar_i * lam_i) / den
    f_i = (abar_i * lam_r - nr * lam_i) / den
    br = b_re.astype(f32)
    bi = b_im.astype(f32)
    bbar_r = f_r[..., None] * br - f_i[..., None] * bi
    bbar_i = f_r[..., None] * bi + f_i[..., None] * br
    bu_r = jnp.einsum('blgp,gnp->blgn', ug, bbar_r)
    bu_i = jnp.einsum('blgp,gnp->blgn', ug, bbar_i)
    ones_t = jnp.ones((1, seqlen, 1, 1), f32)
    a_seq_r = abar_r[None, None] * ones_t
    a_seq_i = abar_i[None, None] * ones_t
    _, _, s_r, s_i = lax.associative_scan(
        _complex_linear_combine, (a_seq_r, a_seq_i, bu_r, bu_i), axis=1)
    y = (jnp.einsum('gpn,blgn->blgp', c_re.astype(f32), s_r)
         - jnp.einsum('gpn,blgn->blgp', c_im.astype(f32), s_i))
    y = y.reshape(bsz, seqlen, SSM_WIDTH) + d_skip.astype(f32) * uf
    y = jax.nn.gelu(y)
    y = y * jax.nn.sigmoid(y @ w_glu.astype(f32) + b_glu.astype(f32))
    return y.astype(u.dtype)


def pool_mixer(v, w_pool, pool_scale):
    f32 = jnp.float32
    bsz, seqlen, _ = v.shape
    vg = v.astype(f32).reshape(bsz, seqlen, POOL_GROUPS, POOL_GROUP_WIDTH)
    csum = jnp.cumsum(jnp.pad(vg, ((0, 0), (POOL_PAD, 0), (0, 0), (0, 0))), axis=1)
    t = jnp.arange(seqlen)
    outs = []
    for gi, w in enumerate(POOL_WINDOWS):
        win_sum = csum[:, POOL_PAD:, gi] - csum[:, POOL_PAD - w:POOL_PAD - w + seqlen, gi]
        count = jnp.minimum(t + 1, w).astype(f32)[None, :, None]
        outs.append(win_sum / count - vg[:, :, gi])
    pooled = jnp.stack(outs, axis=2)
    y = jnp.einsum('blgc,gcd->blgd', pooled, w_pool.astype(f32))
    y = y.reshape(bsz, seqlen, POOL_WIDTH) * pool_scale.astype(f32)
    return y.astype(v.dtype)


def setup_inputs(seed: int = 0) -> dict:
    key = jax.random.key(seed)
    ks = iter(jax.random.split(key, 48))
    f32 = jnp.float32

    def nrm(shape, scale):
        return jax.random.normal(next(ks), shape, f32) * scale

    def gain(shape):
        return 1.0 + nrm(shape, 0.02)

    x = nrm((BATCH, SEQ, D_MODEL), 1.0)
    p = nrm((DEPTH, BATCH, SEQ, PLE_DIM), 1.0)
    ffn1_norm = gain((DEPTH, D_MODEL))
    ffn1_w1 = nrm((DEPTH, D_MODEL, D_FF), D_MODEL ** -0.5)
    ffn1_w3 = nrm((DEPTH, D_MODEL, D_FF), D_MODEL ** -0.5)
    ffn1_w2 = nrm((DEPTH, D_FF, D_MODEL), D_FF ** -0.5)
    mix_norm = gain((DEPTH, D_MODEL))
    w_in = nrm((DEPTH, D_MODEL, MIX_WIDTH), D_MODEL ** -0.5)
    a_re = -0.5 + nrm((DEPTH, SSM_GROUPS, SSM_STATE), 0.01)
    a_im = math.pi * jnp.arange(SSM_STATE, dtype=f32)[None, None, :] + nrm((DEPTH, SSM_GROUPS, SSM_STATE), 0.01)
    log_dt = jax.random.uniform(next(ks), (DEPTH, SSM_GROUPS), f32, math.log(DT_MIN), math.log(DT_MAX))
    b_re = nrm((DEPTH, SSM_GROUPS, SSM_STATE, SSM_GROUP), (2 * SSM_GROUP) ** -0.5)
    b_im = nrm((DEPTH, SSM_GROUPS, SSM_STATE, SSM_GROUP), (2 * SSM_GROUP) ** -0.5)
    c_re = nrm((DEPTH, SSM_GROUPS, SSM_GROUP, SSM_STATE), (2 * SSM_STATE) ** -0.5)
    c_im = nrm((DEPTH, SSM_GROUPS, SSM_GROUP, SSM_STATE), (2 * SSM_STATE) ** -0.5)
    d_skip = nrm((DEPTH, SSM_WIDTH), 1.0)
    w_glu = nrm((DEPTH, SSM_WIDTH, SSM_WIDTH), SSM_WIDTH ** -0.5)
    b_glu = nrm((DEPTH, SSM_WIDTH), 0.01)
    w_pool = nrm((DEPTH, POOL_GROUPS, POOL_GROUP_WIDTH, POOL_GROUP_WIDTH), POOL_GROUP_WIDTH ** -0.5)
    pool_scale = gain((DEPTH, POOL_WIDTH))
    ssm_out_norm = gain((DEPTH, SSM_WIDTH))
    pool_out_norm = gain((DEPTH, POOL_WIDTH))
    w_out = nrm((DEPTH, MIX_WIDTH, D_MODEL), MIX_WIDTH ** -0.5)
    ffn2_norm = gain((DEPTH, D_MODEL))
    ffn2_w1 = nrm((DEPTH, D_MODEL, D_FF), D_MODEL ** -0.5)
    ffn2_w3 = nrm((DEPTH, D_MODEL, D_FF), D_MODEL ** -0.5)
    ffn2_w2 = nrm((DEPTH, D_FF, D_MODEL), D_FF ** -0.5)
    ple_gate_norm = gain((DEPTH, D_MODEL))
    w_ple_gate = nrm((DEPTH, D_MODEL, D_MODEL), D_MODEL ** -0.5)
    w_ple_proj = nrm((DEPTH, PLE_DIM, D_MODEL), PLE_DIM ** -0.5)
    ple_norm = gain((DEPTH, D_MODEL))
    final_norm = gain((D_MODEL,))
    return {'x': x, 'p': p,
            'ffn1_norm': ffn1_norm, 'ffn1_w1': ffn1_w1, 'ffn1_w3': ffn1_w3, 'ffn1_w2': ffn1_w2,
            'mix_norm': mix_norm, 'w_in': w_in,
            'a_re': a_re, 'a_im': a_im, 'log_dt': log_dt, 'b_re': b_re, 'b_im': b_im,
            'c_re': c_re, 'c_im': c_im, 'd_skip': d_skip, 'w_glu': w_glu, 'b_glu': b_glu,
            'w_pool': w_pool, 'pool_scale': pool_scale,
            'ssm_out_norm': ssm_out_norm, 'pool_out_norm': pool_out_norm, 'w_out': w_out,
            'ffn2_norm': ffn2_norm, 'ffn2_w1': ffn2_w1, 'ffn2_w3': ffn2_w3, 'ffn2_w2': ffn2_w2,
            'ple_gate_norm': ple_gate_norm, 'w_ple_gate': w_ple_gate, 'w_ple_proj': w_ple_proj,
            'ple_norm': ple_norm, 'final_norm': final_norm}


def reference(x, p, ffn1_norm, ffn1_w1, ffn1_w3, ffn1_w2, mix_norm, w_in,
              a_re, a_im, log_dt, b_re, b_im, c_re, c_im, d_skip, w_glu, b_glu,
              w_pool, pool_scale, ssm_out_norm, pool_out_norm, w_out,
              ffn2_norm, ffn2_w1, ffn2_w3, ffn2_w2,
              ple_gate_norm, w_ple_gate, w_ple_proj, ple_norm, final_norm):
    h = x
    for i in range(DEPTH):
        h = h + FFN_RES * swiglu(rmsnorm(h, ffn1_norm[i]), ffn1_w1[i], ffn1_w3[i], ffn1_w2[i])
        z = rmsnorm(h, mix_norm[i]) @ w_in[i]
        y_ssm = s5_mixer(z[..., :SSM_WIDTH], a_re[i], a_im[i], log_dt[i], b_re[i], b_im[i],
                         c_re[i], c_im[i], d_skip[i], w_glu[i], b_glu[i])
        y_pool = pool_mixer(z[..., SSM_WIDTH:], w_pool[i], pool_scale[i])
        y = jnp.concatenate([rmsnorm(y_ssm, ssm_out_norm[i]),
                             rmsnorm(y_pool, pool_out_norm[i])], axis=-1)
        h = h + y @ w_out[i]
        h = h + FFN_RES * swiglu(rmsnorm(h, ffn2_norm[i]), ffn2_w1[i], ffn2_w3[i], ffn2_w2[i])
        gate = jax.nn.sigmoid(rmsnorm(h, ple_gate_norm[i]) @ w_ple_gate[i])
        emb = rmsnorm(p[i].astype(h.dtype) @ w_ple_proj[i], ple_norm[i])
        h = h + gate * emb
    return rmsnorm(h, final_norm)
```

```python
import functools
import math

import jax
import jax.numpy as jnp
import numpy as np
from jax import lax
from jax.experimental import pallas as pl
from jax.experimental.pallas import tpu as pltpu

F32 = jnp.float32
BF16 = jnp.bfloat16

D_MODEL = 1024
D_FF = 2816
PLE_DIM = 256
SSM_WIDTH = 512
POOL_WIDTH = 512
SSM_GROUP = 16
SSM_GROUPS = 32
SSM_STATE = 64
POOL_WINDOWS = (2, 4, 8, 16)
POOL_GROUP_WIDTH = 128
FFN_RES = 0.5
EPS = 1e-6

LANES = 128
FOLD = LANES // SSM_GROUP
PAIRS = SSM_GROUPS // 2
PAIR_W = 2 * LANES
STATE_W = PAIRS * PAIR_W
GROUPS_PER_VREG = LANES // SSM_GROUP
SSM_VREGS = SSM_WIDTH // LANES

TM = 512
TL = 512
ROWS = TL // FOLD
FF_CHUNK = 256
POOL_TAIL = 32
VMEM_LIMIT = 56 * 1024 * 1024


def _rms(x, g):
    return x * lax.rsqrt(jnp.mean(x * x, axis=-1, keepdims=True) + EPS) * g


def _dot(a, b):
    return jnp.dot(a, b, preferred_element_type=F32)


def _swiglu(xn, w1_ref, w3_ref, w2_ref, act_ref):
    for c in range(D_FF // FF_CHUNK):
        sl = slice(c * FF_CHUNK, (c + 1) * FF_CHUNK)
        g = _dot(xn, w1_ref[:, sl])
        u = _dot(xn, w3_ref[:, sl])
        act_ref[:, sl] = (g * jax.nn.sigmoid(g) * u).astype(BF16)
    return _dot(act_ref[...], w2_ref[...])


def _ffn1_kernel(x_ref, norm_ref, w1_ref, w3_ref, w2_ref, o_ref, act_ref):
    x = x_ref[...]
    xn = _rms(x, norm_ref[...]).astype(BF16)
    o_ref[...] = x + FFN_RES * _swiglu(xn, w1_ref, w3_ref, w2_ref, act_ref)


def _ffn2_kernel(h_ref, p_ref, norm_ref, w1_ref, w3_ref, w2_ref, gnorm_ref, wg_ref, wp_ref,
                 pnorm_ref, fnorm_ref, o_ref, act_ref):
    h = h_ref[...]
    xn = _rms(h, norm_ref[...]).astype(BF16)
    h = h + FFN_RES * _swiglu(xn, w1_ref, w3_ref, w2_ref, act_ref)
    gate = jax.nn.sigmoid(_dot(_rms(h, gnorm_ref[...]).astype(BF16), wg_ref[...]))
    emb = _rms(_dot(p_ref[...].astype(BF16), wp_ref[...]), pnorm_ref[...])
    h = h + gate * emb
    o_ref[...] = _rms(h, fnorm_ref[...])


def _shift_rows(x, d, row):
    return jnp.where(row >= d, pltpu.roll(x, d, 0), 0.0)


def _mixer_kernel(h_ref, mnorm_ref, win_ref, bsup_ref, tsup_ref, csup_ref, alev_ref, acar_ref,
                  dskip_ref, wglu_ref, bglu_ref, wpool_ref, pscale_ref, snorm_ref, pnorm_ref,
                  wout_ref, o_ref,
                  state_ref, zs_ref, ys_ref, vbuf_ref, s2_ref, s4_ref, s8_ref):
    l = pl.program_id(1)
    nr = TL + POOL_TAIL

    @pl.when(l == 0)
    def _():
        state_ref[...] = jnp.zeros_like(state_ref)
        vbuf_ref[0:POOL_TAIL, :] = jnp.zeros((POOL_TAIL, POOL_WIDTH), F32)

    @pl.when(l > 0)
    def _():
        vbuf_ref[0:POOL_TAIL, :] = vbuf_ref[TL:nr, :]

    h = h_ref[...]
    z = _dot(_rms(h, mnorm_ref[...]).astype(BF16), win_ref[...])

    for v in range(SSM_VREGS):
        zs_ref[v] = z[:, v * LANES:(v + 1) * LANES]
    vbuf_ref[POOL_TAIL:nr, :] = z[:, SSM_WIDTH:]

    lane_win = lax.broadcasted_iota(jnp.int32, (ROWS, LANES), 1) // SSM_GROUP
    row = lax.broadcasted_iota(jnp.int32, (ROWS, LANES), 0)
    levels = [1 << i for i in range(int(math.log2(ROWS)))]

    for v in range(SSM_VREGS):
        toks = [zs_ref[v, pl.ds(j, ROWS, stride=FOLD), :] for j in range(FOLD)]
        ysup = []
        for pp in range(GROUPS_PER_VREG // 2):
            pair = v * (GROUPS_PER_VREG // 2) + pp
            halves = []
            for gl in range(2):
                i = 2 * pp + gl
                folded = pltpu.roll(toks[0], (-SSM_GROUP * i) % LANES, 1)
                for j in range(1, FOLD):
                    piece = pltpu.roll(toks[j], (SSM_GROUP * (j - i)) % LANES, 1)
                    folded = jnp.where(lane_win == j, piece, folded)
                halves.append(folded)
            usup = jnp.concatenate(halves, axis=1).astype(BF16)

            x = _dot(usup, bsup_ref[pair])
            xr, xi = x[:, :LANES], x[:, LANES:]
            base = pair * PAIR_W
            for lev, d in enumerate(levels):
                ar = alev_ref[lev:lev + 1, base:base + LANES]
                ai = alev_ref[lev:lev + 1, base + LANES:base + PAIR_W]
                sr = _shift_rows(xr, d, row)
                si = _shift_rows(xi, d, row)
                xr, xi = xr + ar * sr - ai * si, xi + ar * si + ai * sr
            cr = state_ref[0:1, base:base + LANES]
            ci = state_ref[0:1, base + LANES:base + PAIR_W]
            tr = acar_ref[:, base:base + LANES]
            ti = acar_ref[:, base + LANES:base + PAIR_W]
            sr = xr + tr * cr - ti * ci
            si = xi + tr * ci + ti * cr
            state_ref[0:1, base:base + LANES] = sr[ROWS - 1:ROWS, :]
            state_ref[0:1, base + LANES:base + PAIR_W] = si[ROWS - 1:ROWS, :]
            pr = jnp.where(row == 0, cr, pltpu.roll(sr, 1, 0))
            pi = jnp.where(row == 0, ci, pltpu.roll(si, 1, 0))
            sprev = jnp.concatenate([pr, pi], axis=1).astype(BF16)
            y = _dot(usup, tsup_ref[pair]) + _dot(sprev, csup_ref[pair])
            ysup.append(y[:, :LANES])
            ysup.append(y[:, LANES:])
        for j in range(FOLD):
            out = pltpu.roll(ysup[0], (SSM_GROUP * (0 - j)) % LANES, 1)
            for i in range(1, GROUPS_PER_VREG):
                piece = pltpu.roll(ysup[i], (SSM_GROUP * (i - j)) % LANES, 1)
                out = jnp.where(lane_win == i, piece, out)
            ys_ref[v, pl.ds(j, ROWS, stride=FOLD), :] = out

    u = z[:, :SSM_WIDTH]
    y = jnp.concatenate([ys_ref[v] for v in range(SSM_VREGS)], axis=1) + dskip_ref[...] * u
    y = jax.nn.gelu(y, approximate=True)
    y = y * jax.nn.sigmoid(_dot(y.astype(BF16), wglu_ref[...]) + bglu_ref[...])
    yn_ssm = _rms(y, snorm_ref[...]).astype(BF16)

    s2_ref[8:nr, :] = vbuf_ref[8:nr, :] + vbuf_ref[7:nr - 1, :]
    s4_ref[16:nr, :] = s2_ref[16:nr, LANES:] + s2_ref[14:nr - 2, LANES:]
    s8_ref[24:nr, :] = s4_ref[24:nr, LANES:] + s4_ref[20:nr - 4, LANES:]
    s16 = s8_ref[32:nr, LANES:] + s8_ref[24:nr - 8, LANES:]
    wins = [s2_ref[POOL_TAIL:nr, 0:LANES], s4_ref[POOL_TAIL:nr, 0:LANES],
            s8_ref[POOL_TAIL:nr, 0:LANES], s16]
    t = l * TL + lax.broadcasted_iota(jnp.int32, (TL, 1), 0)
    pooled = []
    for gi, w in enumerate(POOL_WINDOWS):
        inv_count = 1.0 / jnp.minimum(t + 1, w).astype(F32)
        vg = vbuf_ref[POOL_TAIL:nr, gi * LANES:(gi + 1) * LANES]
        pg = (wins[gi] * inv_count - vg).astype(BF16)
        pooled.append(_dot(pg, wpool_ref[gi]))
    y_pool = jnp.concatenate(pooled, axis=1) * pscale_ref[...]
    yn_pool = _rms(y_pool, pnorm_ref[...]).astype(BF16)

    o_ref[...] = (h + _dot(yn_ssm, wout_ref[0:SSM_WIDTH, :])
                  + _dot(yn_pool, wout_ref[SSM_WIDTH:, :]))


def _pair_lanes(re, im):
    lead = re.shape[:-2]
    re = re.reshape(lead + (PAIRS, 2 * SSM_STATE))
    im = im.reshape(lead + (PAIRS, 2 * SSM_STATE))
    return jnp.concatenate([re, im], axis=-1).reshape(lead + (STATE_W,))


def _ssm_tables(a_re, a_im, log_dt, b_re, b_im, c_re, c_im):
    hi = lax.Precision.HIGHEST
    dt = jnp.exp(log_dt)[:, None]

    def apow(m):
        m = jnp.asarray(m, F32)[:, None, None]
        mag = jnp.exp(a_re * dt * m)
        return mag * jnp.cos(a_im * dt * m), mag * jnp.sin(a_im * dt * m)

    ar, ai = apow([1.0])
    ar, ai = ar[0], ai[0]
    den = a_re * a_re + a_im * a_im
    nr = ar - 1.0
    f_r = (nr * a_re + ai * a_im) / den
    f_i = (ai * a_re - nr * a_im) / den
    bb_r = f_r[..., None] * b_re - f_i[..., None] * b_im
    bb_i = f_r[..., None] * b_im + f_i[..., None] * b_re

    qr, qi = apow(np.arange(FOLD - 1, -1, -1))
    qr = jnp.transpose(qr, (1, 0, 2))[:, :, None, :]
    qi = jnp.transpose(qi, (1, 0, 2))[:, :, None, :]
    bt_r = jnp.transpose(bb_r, (0, 2, 1))[:, None]
    bt_i = jnp.transpose(bb_i, (0, 2, 1))[:, None]
    bs_r = (qr * bt_r - qi * bt_i).reshape(PAIRS, 2, LANES, SSM_STATE)
    bs_i = (qr * bt_i + qi * bt_r).reshape(PAIRS, 2, LANES, SSM_STATE)
    zb = jnp.zeros((PAIRS, LANES, SSM_STATE), F32)
    bsup = jnp.concatenate([
        jnp.concatenate([bs_r[:, 0], zb, bs_i[:, 0], zb], axis=-1),
        jnp.concatenate([zb, bs_r[:, 1], zb, bs_i[:, 1]], axis=-1)], axis=1)

    er, ei = apow(np.arange(1, FOLD + 1))
    er = jnp.transpose(er, (1, 0, 2))[:, :, None, :]
    ei = jnp.transpose(ei, (1, 0, 2))[:, :, None, :]
    m_r = (c_re[:, None] * er - c_im[:, None] * ei).reshape(SSM_GROUPS, LANES, SSM_STATE)
    m_i = (c_re[:, None] * ei + c_im[:, None] * er).reshape(SSM_GROUPS, LANES, SSM_STATE)
    ct_r = jnp.transpose(m_r, (0, 2, 1)).reshape(PAIRS, 2, SSM_STATE, LANES)
    ct_i = -jnp.transpose(m_i, (0, 2, 1)).reshape(PAIRS, 2, SSM_STATE, LANES)
    zc = jnp.zeros((PAIRS, SSM_STATE, LANES), F32)
    csup = jnp.concatenate([
        jnp.concatenate([ct_r[:, 0], zc], axis=-1),
        jnp.concatenate([zc, ct_r[:, 1]], axis=-1),
        jnp.concatenate([ct_i[:, 0], zc], axis=-1),
        jnp.concatenate([zc, ct_i[:, 1]], axis=-1)], axis=1)

    kr, ki = apow(np.arange(FOLD))
    kr = jnp.transpose(kr, (1, 0, 2))[:, :, None, :]
    ki = jnp.transpose(ki, (1, 0, 2))[:, :, None, :]
    w_r = c_re[:, None] * kr - c_im[:, None] * ki
    w_i = c_re[:, None] * ki + c_im[:, None] * kr
    kern = (jnp.einsum('gtpn,gnq->gtpq', w_r, bb_r, precision=hi)
            - jnp.einsum('gtpn,gnq->gtpq', w_i, bb_i, precision=hi))
    lag = np.arange(FOLD)[None, :] - np.arange(FOLD)[:, None]
    toe = jnp.take(kern, jnp.asarray(np.maximum(lag, 0).reshape(-1)), axis=1)
    toe = toe.reshape(SSM_GROUPS, FOLD, FOLD, SSM_GROUP, SSM_GROUP)
    toe = jnp.where(jnp.asarray(lag >= 0)[None, :, :, None, None], toe, 0.0)
    toe = jnp.transpose(toe, (0, 1, 4, 2, 3)).reshape(PAIRS, 2, LANES, LANES)
    zt = jnp.zeros((PAIRS, LANES, LANES), F32)
    tsup = jnp.concatenate([
        jnp.concatenate([toe[:, 0], zt], axis=-1),
        jnp.concatenate([zt, toe[:, 1]], axis=-1)], axis=1)

    nlev = int(math.log2(ROWS))
    alev = _pair_lanes(*apow(FOLD * (2.0 ** np.arange(nlev))))
    acar = _pair_lanes(*apow(FOLD * np.arange(1, ROWS + 1)))
    return bsup.astype(BF16), tsup.astype(BF16), csup.astype(BF16), alev, acar


def _const_spec(shape):
    nd = len(shape)
    return pl.BlockSpec(shape, lambda *_: (0,) * nd)


def _params(n_axes):
    return pltpu.CompilerParams(dimension_semantics=("arbitrary",) * n_axes,
                                vmem_limit_bytes=VMEM_LIMIT)


def kernel(x, p, ffn1_norm, ffn1_w1, ffn1_w3, ffn1_w2, mix_norm, w_in, a_re, a_im, log_dt, b_re, b_im,
           c_re, c_im, d_skip, w_glu, b_glu, w_pool, pool_scale, ssm_out_norm, pool_out_norm, w_out,
           ffn2_norm, ffn2_w1, ffn2_w3, ffn2_w2, ple_gate_norm, w_ple_gate, w_ple_proj, ple_norm,
           final_norm):
    bsz, seqlen, _ = x.shape
    depth = ffn1_w1.shape[0]
    n_tok = bsz * seqlen
    assert n_tok % TM == 0 and seqlen % TL == 0

    row = lambda a: a.reshape(1, -1).astype(F32)
    tok_spec = pl.BlockSpec((TM, D_MODEL), lambda i: (i, 0))
    seq_spec = pl.BlockSpec((None, TL, D_MODEL), lambda b, l: (b, l, 0))

    assert depth == 1
    i = 0

    h = pl.pallas_call(
        _ffn1_kernel,
        grid=(n_tok // TM,),
        in_specs=[tok_spec, _const_spec((1, D_MODEL)), _const_spec((D_MODEL, D_FF)),
                  _const_spec((D_MODEL, D_FF)), _const_spec((D_FF, D_MODEL))],
        out_specs=tok_spec,
        out_shape=jax.ShapeDtypeStruct((n_tok, D_MODEL), F32),
        scratch_shapes=[pltpu.VMEM((TM, D_FF), BF16)],
        compiler_params=_params(1),
        name="ffn1",
    )(x.reshape(n_tok, D_MODEL), row(ffn1_norm[i]), ffn1_w1[i].astype(BF16),
      ffn1_w3[i].astype(BF16), ffn1_w2[i].astype(BF16))

    bsup, tsup, csup, alev, acar = _ssm_tables(a_re[i], a_im[i], log_dt[i], b_re[i], b_im[i],
                                               c_re[i], c_im[i])
    consts = [row(mix_norm[i]), w_in[i].astype(BF16), bsup, tsup, csup, alev, acar,
              row(d_skip[i]), w_glu[i].astype(BF16), row(b_glu[i]), w_pool[i].astype(BF16),
              row(pool_scale[i]), row(ssm_out_norm[i]), row(pool_out_norm[i]),
              w_out[i].astype(BF16)]
    nr = TL + POOL_TAIL
    h = pl.pallas_call(
        _mixer_kernel,
        grid=(bsz, seqlen // TL),
        in_specs=[seq_spec] + [_const_spec(c.shape) for c in consts],
        out_specs=seq_spec,
        out_shape=jax.ShapeDtypeStruct((bsz, seqlen, D_MODEL), F32),
        scratch_shapes=[pltpu.VMEM((8, STATE_W), F32),
                        pltpu.VMEM((SSM_VREGS, TL, LANES), F32),
                        pltpu.VMEM((SSM_VREGS, TL, LANES), F32),
                        pltpu.VMEM((nr, POOL_WIDTH), F32),
                        pltpu.VMEM((nr, POOL_WIDTH), F32),
                        pltpu.VMEM((nr, POOL_WIDTH - LANES), F32),
                        pltpu.VMEM((nr, POOL_WIDTH - 2 * LANES), F32)],
        compiler_params=_params(2),
        name="mixer",
    )(h.reshape(bsz, seqlen, D_MODEL), *consts)

    consts = [row(ffn2_norm[i]), ffn2_w1[i].astype(BF16), ffn2_w3[i].astype(BF16),
              ffn2_w2[i].astype(BF16), row(ple_gate_norm[i]), w_ple_gate[i].astype(BF16),
              w_ple_proj[i].astype(BF16), row(ple_norm[i]), row(final_norm)]
    h = pl.pallas_call(
        _ffn2_kernel,
        grid=(n_tok // TM,),
        in_specs=[tok_spec, pl.BlockSpec((TM, PLE_DIM), lambda t: (t, 0))]
                 + [_const_spec(c.shape) for c in consts],
        out_specs=tok_spec,
        out_shape=jax.ShapeDtypeStruct((n_tok, D_MODEL), F32),
        scratch_shapes=[pltpu.VMEM((TM, D_FF), BF16)],
        compiler_params=_params(1),
        name="ffn2",
    )(h.reshape(n_tok, D_MODEL), p[i].reshape(n_tok, PLE_DIM), *consts)
    return h.reshape(bsz, seqlen, D_MODEL)
```

```python
import jax
import jax.numpy as jnp
import numpy as np
from jax import lax
from jax.experimental import pallas as pl
from jax.experimental.pallas import tpu as pltpu

F32 = jnp.float32
BF16 = jnp.bfloat16

D_MODEL = 1024
D_FF = 2816
PLE_DIM = 256
SSM_WIDTH = 512
POOL_WIDTH = 512
SSM_GROUP = 16
SSM_GROUPS = 32
SSM_STATE = 64
POOL_WINDOWS = (2, 4, 8, 16)
FFN_RES = 0.5
EPS = 1e-6

LANES = 128
SUBLANES = 8
FOLD = LANES // SSM_GROUP
PAIRS = SSM_GROUPS // 2
PAIR_W = 2 * LANES
STATE_W = PAIRS * PAIR_W
SSM_VREGS = SSM_WIDTH // LANES

TM = 512
TT = 128
KROWS = TT // FOLD
PITCH = TT + SUBLANES
FF_CHUNK = 256
POOL_TAIL = 32
VMEM_LIMIT = 56 * 1024 * 1024


def _rms(x, g):
    return x * lax.rsqrt(jnp.mean(x * x, axis=-1, keepdims=True) + EPS) * g


def _dot(a, b):
    return jnp.dot(a, b, preferred_element_type=F32)


def _swiglu(xn, w1_ref, w3_ref, w2_ref, act_ref):
    for c in range(D_FF // FF_CHUNK):
        sl = slice(c * FF_CHUNK, (c + 1) * FF_CHUNK)
        g = _dot(xn, w1_ref[:, sl])
        u = _dot(xn, w3_ref[:, sl])
        act_ref[:, sl] = (g * jax.nn.sigmoid(g) * u).astype(BF16)
    return _dot(act_ref[...], w2_ref[...])


def _ffn1_kernel(x_ref, norm_ref, w1_ref, w3_ref, w2_ref, o_ref, act_ref):
    x = x_ref[...]
    xn = _rms(x, norm_ref[...]).astype(BF16)
    o_ref[...] = x + FFN_RES * _swiglu(xn, w1_ref, w3_ref, w2_ref, act_ref)


def _ffn2_kernel(h_ref, p_ref, norm_ref, w1_ref, w3_ref, w2_ref, gnorm_ref, wg_ref, wp_ref,
                 pnorm_ref, fnorm_ref, o_ref, act_ref):
    h = h_ref[...]
    xn = _rms(h, norm_ref[...]).astype(BF16)
    h = h + FFN_RES * _swiglu(xn, w1_ref, w3_ref, w2_ref, act_ref)
    gate = jax.nn.sigmoid(_dot(_rms(h, gnorm_ref[...]).astype(BF16), wg_ref[...]))
    emb = _rms(_dot(p_ref[...].astype(BF16), wp_ref[...]), pnorm_ref[...])
    h = h + gate * emb
    o_ref[...] = _rms(h, fnorm_ref[...])


def _block_transpose(xs, lane_blk):
    xs = list(xs)
    for s in (4, 2, 1):
        upper = (lane_blk & s) != 0
        nxt = list(xs)
        for r in range(len(xs)):
            if r & s:
                continue
            a, b = xs[r], xs[r + s]
            nxt[r] = jnp.where(upper, pltpu.roll(b, SSM_GROUP * s, 1), a)
            nxt[r + s] = jnp.where(upper, b, pltpu.roll(a, LANES - SSM_GROUP * s, 1))
        xs = nxt
    return xs


def _mixer_kernel(h_ref, mnorm_ref, win_ref, btsup_ref, csup_ref, a8_ref,
                  dskip_ref, wglu_ref, bglu_ref, wpool_ref, pscale_ref, snorm_ref, pnorm_ref,
                  wout_ref, o_ref,
                  state_ref, zs_ref, ys_ref, vbuf_ref, s2_ref, s4_ref, s8_ref):
    l = pl.program_id(0)
    nb = h_ref.shape[0]
    nr = TT + POOL_TAIL
    rows = nb * KROWS

    @pl.when(l == 0)
    def _():
        state_ref[...] = jnp.zeros_like(state_ref)
        vbuf_ref[:, 0:POOL_TAIL, :] = jnp.zeros((nb, POOL_TAIL, POOL_WIDTH), F32)

    @pl.when(l > 0)
    def _():
        vbuf_ref[:, 0:POOL_TAIL, :] = vbuf_ref[:, TT:nr, :]

    h = h_ref[...].reshape(nb * TT, D_MODEL)
    z = _dot(_rms(h, mnorm_ref[...]).astype(BF16), win_ref[...])

    for b in range(nb):
        for v in range(SSM_VREGS):
            zs_ref[v, b * PITCH:b * PITCH + TT, :] = z[b * TT:(b + 1) * TT, v * LANES:(v + 1) * LANES]
        vbuf_ref[b, POOL_TAIL:nr, :] = z[b * TT:(b + 1) * TT, SSM_WIDTH:]

    lane_blk = lax.broadcasted_iota(jnp.int32, (rows, LANES), 1) // SSM_GROUP
    for v in range(SSM_VREGS):
        toks = [jnp.concatenate([zs_ref[v, pl.ds(FOLD * k + j, nb, stride=PITCH), :]
                                 for k in range(KROWS)], axis=0) for j in range(FOLD)]
        folded = _block_transpose(toks, lane_blk)
        ysup = []
        for pp in range(FOLD // 2):
            pair = v * (FOLD // 2) + pp
            base = pair * PAIR_W
            usup = jnp.concatenate([folded[2 * pp], folded[2 * pp + 1]], axis=1).astype(BF16)
            xy = _dot(usup, btsup_ref[pair])
            ar = jnp.broadcast_to(a8_ref[0:1, base:base + LANES], (nb, LANES))
            ai = jnp.broadcast_to(a8_ref[0:1, base + LANES:base + PAIR_W], (nb, LANES))
            sr = state_ref[:, base:base + LANES]
            si = state_ref[:, base + LANES:base + PAIR_W]
            prev_r, prev_i = [], []
            for k in range(KROWS):
                prev_r.append(sr)
                prev_i.append(si)
                xr = xy[k * nb:(k + 1) * nb, 0:LANES]
                xi = xy[k * nb:(k + 1) * nb, LANES:PAIR_W]
                sr, si = ar * sr - ai * si + xr, ar * si + ai * sr + xi
            state_ref[:, base:base + LANES] = sr
            state_ref[:, base + LANES:base + PAIR_W] = si
            sprev = jnp.concatenate([jnp.concatenate(prev_r, axis=0),
                                     jnp.concatenate(prev_i, axis=0)], axis=1).astype(BF16)
            y = xy[:, PAIR_W:] + _dot(sprev, csup_ref[pair])
            ysup.append(y[:, :LANES])
            ysup.append(y[:, LANES:])
        outs = _block_transpose(ysup, lane_blk)
        for j in range(FOLD):
            for k in range(KROWS):
                ys_ref[v, pl.ds(FOLD * k + j, nb, stride=PITCH), :] = outs[j][k * nb:(k + 1) * nb, :]

    y = jnp.concatenate(
        [jnp.concatenate([ys_ref[v, b * PITCH:b * PITCH + TT, :] for v in range(SSM_VREGS)], axis=1)
         for b in range(nb)], axis=0)
    y = y + dskip_ref[...] * z[:, :SSM_WIDTH]
    y = jax.nn.gelu(y, approximate=True)
    y = y * jax.nn.sigmoid(_dot(y.astype(BF16), wglu_ref[...]) + bglu_ref[...])
    yn_ssm = _rms(y, snorm_ref[...]).astype(BF16)

    s2_ref[:, 8:nr, :] = vbuf_ref[:, 8:nr, :] + vbuf_ref[:, 7:nr - 1, :]
    s4_ref[:, 16:nr, :] = s2_ref[:, 16:nr, LANES:] + s2_ref[:, 14:nr - 2, LANES:]
    s8_ref[:, 24:nr, :] = s4_ref[:, 24:nr, LANES:] + s4_ref[:, 20:nr - 4, LANES:]
    s16 = s8_ref[:, 32:nr, LANES:] + s8_ref[:, 24:nr - 8, LANES:]
    wins = [s2_ref[:, POOL_TAIL:nr, 0:LANES], s4_ref[:, POOL_TAIL:nr, 0:LANES],
            s8_ref[:, POOL_TAIL:nr, 0:LANES], s16]
    t = l * TT + lax.broadcasted_iota(jnp.int32, (1, TT, 1), 1)
    pooled = []
    for gi, w in enumerate(POOL_WINDOWS):
        inv_count = 1.0 / jnp.minimum(t + 1, w).astype(F32)
        vg = vbuf_ref[:, POOL_TAIL:nr, gi * LANES:(gi + 1) * LANES]
        pg = (wins[gi] * inv_count - vg).reshape(nb * TT, LANES).astype(BF16)
        pooled.append(_dot(pg, wpool_ref[gi]))
    y_pool = jnp.concatenate(pooled, axis=1) * pscale_ref[...]
    yn_pool = _rms(y_pool, pnorm_ref[...]).astype(BF16)

    out = h + _dot(yn_ssm, wout_ref[0:SSM_WIDTH, :]) + _dot(yn_pool, wout_ref[SSM_WIDTH:, :])
    o_ref[...] = out.reshape(nb, TT, D_MODEL)


def _pair_lanes(re, im):
    lead = re.shape[:-2]
    re = re.reshape(lead + (PAIRS, 2 * SSM_STATE))
    im = im.reshape(lead + (PAIRS, 2 * SSM_STATE))
    return jnp.concatenate([re, im], axis=-1).reshape(lead + (STATE_W,))


def _ssm_tables(a_re, a_im, log_dt, b_re, b_im, c_re, c_im):
    hi = lax.Precision.HIGHEST
    dt = jnp.exp(log_dt)[:, None]

    def apow(m):
        m = jnp.asarray(m, F32)[:, None, None]
        mag = jnp.exp(a_re * dt * m)
        return mag * jnp.cos(a_im * dt * m), mag * jnp.sin(a_im * dt * m)

    ar, ai = apow([1.0])
    ar, ai = ar[0], ai[0]
    den = a_re * a_re + a_im * a_im
    nr = ar - 1.0
    f_r = (nr * a_re + ai * a_im) / den
    f_i = (ai * a_re - nr * a_im) / den
    bb_r = f_r[..., None] * b_re - f_i[..., None] * b_im
    bb_i = f_r[..., None] * b_im + f_i[..., None] * b_re

    qr, qi = apow(np.arange(FOLD - 1, -1, -1))
    qr = jnp.transpose(qr, (1, 0, 2))[:, :, None, :]
    qi = jnp.transpose(qi, (1, 0, 2))[:, :, None, :]
    bt_r = jnp.transpose(bb_r, (0, 2, 1))[:, None]
    bt_i = jnp.transpose(bb_i, (0, 2, 1))[:, None]
    bs_r = (qr * bt_r - qi * bt_i).reshape(PAIRS, 2, LANES, SSM_STATE)
    bs_i = (qr * bt_i + qi * bt_r).reshape(PAIRS, 2, LANES, SSM_STATE)
    zb = jnp.zeros((PAIRS, LANES, SSM_STATE), F32)
    bsup = jnp.concatenate([
        jnp.concatenate([bs_r[:, 0], zb, bs_i[:, 0], zb], axis=-1),
        jnp.concatenate([zb, bs_r[:, 1], zb, bs_i[:, 1]], axis=-1)], axis=1)

    er, ei = apow(np.arange(1, FOLD + 1))
    er = jnp.transpose(er, (1, 0, 2))[:, :, None, :]
    ei = jnp.transpose(ei, (1, 0, 2))[:, :, None, :]
    m_r = (c_re[:, None] * er - c_im[:, None] * ei).reshape(SSM_GROUPS, LANES, SSM_STATE)
    m_i = (c_re[:, None] * ei + c_im[:, None] * er).reshape(SSM_GROUPS, LANES, SSM_STATE)
    ct_r = jnp.transpose(m_r, (0, 2, 1)).reshape(PAIRS, 2, SSM_STATE, LANES)
    ct_i = -jnp.transpose(m_i, (0, 2, 1)).reshape(PAIRS, 2, SSM_STATE, LANES)
    zc = jnp.zeros((PAIRS, SSM_STATE, LANES), F32)
    csup = jnp.concatenate([
        jnp.concatenate([ct_r[:, 0], zc], axis=-1),
        jnp.concatenate([zc, ct_r[:, 1]], axis=-1),
        jnp.concatenate([ct_i[:, 0], zc], axis=-1),
        jnp.concatenate([zc, ct_i[:, 1]], axis=-1)], axis=1)

    kr, ki = apow(np.arange(FOLD))
    kr = jnp.transpose(kr, (1, 0, 2))[:, :, None, :]
    ki = jnp.transpose(ki, (1, 0, 2))[:, :, None, :]
    w_r = c_re[:, None] * kr - c_im[:, None] * ki
    w_i = c_re[:, None] * ki + c_im[:, None] * kr
    kern = (jnp.einsum('gtpn,gnq->gtpq', w_r, bb_r, precision=hi)
            - jnp.einsum('gtpn,gnq->gtpq', w_i, bb_i, precision=hi))
    lag = np.arange(FOLD)[None, :] - np.arange(FOLD)[:, None]
    toe = jnp.take(kern, jnp.asarray(np.maximum(lag, 0).reshape(-1)), axis=1)
    toe = toe.reshape(SSM_GROUPS, FOLD, FOLD, SSM_GROUP, SSM_GROUP)
    toe = jnp.where(jnp.asarray(lag >= 0)[None, :, :, None, None], toe, 0.0)
    toe = jnp.transpose(toe, (0, 1, 4, 2, 3)).reshape(PAIRS, 2, LANES, LANES)
    zt = jnp.zeros((PAIRS, LANES, LANES), F32)
    tsup = jnp.concatenate([
        jnp.concatenate([toe[:, 0], zt], axis=-1),
        jnp.concatenate([zt, toe[:, 1]], axis=-1)], axis=1)

    btsup = jnp.concatenate([bsup, tsup], axis=-1)
    a8 = _pair_lanes(*apow([float(FOLD)]))
    return btsup.astype(BF16), csup.astype(BF16), a8


def _const_spec(shape):
    nd = len(shape)
    return pl.BlockSpec(shape, lambda *_: (0,) * nd)


def _params():
    return pltpu.CompilerParams(dimension_semantics=("arbitrary",), vmem_limit_bytes=VMEM_LIMIT)


def kernel(x, p, ffn1_norm, ffn1_w1, ffn1_w3, ffn1_w2, mix_norm, w_in, a_re, a_im, log_dt, b_re, b_im,
           c_re, c_im, d_skip, w_glu, b_glu, w_pool, pool_scale, ssm_out_norm, pool_out_norm, w_out,
           ffn2_norm, ffn2_w1, ffn2_w3, ffn2_w2, ple_gate_norm, w_ple_gate, w_ple_proj, ple_norm,
           final_norm):
    bsz, seqlen, _ = x.shape
    depth = ffn1_w1.shape[0]
    n_tok = bsz * seqlen
    assert n_tok % TM == 0 and seqlen % TT == 0 and bsz == SUBLANES

    row = lambda a: a.reshape(1, -1).astype(F32)
    tok_spec = pl.BlockSpec((TM, D_MODEL), lambda t: (t, 0))
    seq_spec = pl.BlockSpec((bsz, TT, D_MODEL), lambda l: (0, l, 0))

    assert depth == 1
    i = 0

    h = pl.pallas_call(
        _ffn1_kernel,
        grid=(n_tok // TM,),
        in_specs=[tok_spec, _const_spec((1, D_MODEL)), _const_spec((D_MODEL, D_FF)),
                  _const_spec((D_MODEL, D_FF)), _const_spec((D_FF, D_MODEL))],
        out_specs=tok_spec,
        out_shape=jax.ShapeDtypeStruct((n_tok, D_MODEL), F32),
        scratch_shapes=[pltpu.VMEM((TM, D_FF), BF16)],
        compiler_params=_params(),
        name="ffn1",
    )(x.reshape(n_tok, D_MODEL), row(ffn1_norm[i]), ffn1_w1[i].astype(BF16),
      ffn1_w3[i].astype(BF16), ffn1_w2[i].astype(BF16))

    btsup, csup, a8 = _ssm_tables(a_re[i], a_im[i], log_dt[i], b_re[i], b_im[i], c_re[i], c_im[i])
    consts = [row(mix_norm[i]), w_in[i].astype(BF16), btsup, csup, a8,
              row(d_skip[i]), w_glu[i].astype(BF16), row(b_glu[i]), w_pool[i].astype(BF16),
              row(pool_scale[i]), row(ssm_out_norm[i]), row(pool_out_norm[i]),
              w_out[i].astype(BF16)]
    nr = TT + POOL_TAIL
    h = pl.pallas_call(
        _mixer_kernel,
        grid=(seqlen // TT,),
        in_specs=[seq_spec] + [_const_spec(c.shape) for c in consts],
        out_specs=seq_spec,
        out_shape=jax.ShapeDtypeStruct((bsz, seqlen, D_MODEL), F32),
        scratch_shapes=[pltpu.VMEM((bsz, STATE_W), F32),
                        pltpu.VMEM((SSM_VREGS, bsz * PITCH, LANES), F32),
                        pltpu.VMEM((SSM_VREGS, bsz * PITCH, LANES), F32),
                        pltpu.VMEM((bsz, nr, POOL_WIDTH), F32),
                        pltpu.VMEM((bsz, nr, POOL_WIDTH), F32),
                        pltpu.VMEM((bsz, nr, POOL_WIDTH - LANES), F32),
                        pltpu.VMEM((bsz, nr, POOL_WIDTH - 2 * LANES), F32)],
        compiler_params=_params(),
        name="mixer",
    )(h.reshape(bsz, seqlen, D_MODEL), *consts)

    consts = [row(ffn2_norm[i]), ffn2_w1[i].astype(BF16), ffn2_w3[i].astype(BF16),
              ffn2_w2[i].astype(BF16), row(ple_gate_norm[i]), w_ple_gate[i].astype(BF16),
              w_ple_proj[i].astype(BF16), row(ple_norm[i]), row(final_norm)]
    h = pl.pallas_call(
        _ffn2_kernel,
        grid=(n_tok // TM,),
        in_specs=[tok_spec, pl.BlockSpec((TM, PLE_DIM), lambda t: (t, 0))]
                 + [_const_spec(c.shape) for c in consts],
        out_specs=tok_spec,
        out_shape=jax.ShapeDtypeStruct((n_tok, D_MODEL), F32),
        scratch_shapes=[pltpu.VMEM((TM, D_FF), BF16)],
        compiler_params=_params(),
        name="ffn2",
    )(h.reshape(n_tok, D_MODEL), p[i].reshape(n_tok, PLE_DIM), *consts)
    return h.reshape(bsz, seqlen, D_MODEL)
```

```python
import jax
import jax.numpy as jnp
import numpy as np
from jax import lax
from jax.experimental import pallas as pl
from jax.experimental.pallas import tpu as pltpu

F32 = jnp.float32
BF16 = jnp.bfloat16

D_MODEL = 1024
D_FF = 2816
PLE_DIM = 256
SSM_WIDTH = 512
POOL_WIDTH = 512
SSM_GROUP = 16
SSM_GROUPS = 32
SSM_STATE = 64
POOL_WINDOWS = (2, 4, 8, 16)
FFN_RES = 0.5
EPS = 1e-6

LANES = 128
SUBLANES = 8
FOLD = LANES // SSM_GROUP
PAIRS = SSM_GROUPS // 2
PAIR_W = 2 * LANES
STATE_W = PAIRS * PAIR_W
SSM_VREGS = SSM_WIDTH // LANES

TM = 1024
TT = 128
KROWS = TT // FOLD
PITCH = TT + SUBLANES
FF_CHUNK = 256
POOL_TAIL = 32
VMEM_LIMIT = 56 * 1024 * 1024


def _rms(x, g):
    return x * lax.rsqrt(jnp.mean(x * x, axis=-1, keepdims=True) + EPS) * g


def _dot(a, b):
    return jnp.dot(a, b, preferred_element_type=F32)


def _swiglu(xn, w1_ref, w3_ref, w2_ref, act_ref):
    for c in range(D_FF // FF_CHUNK):
        sl = slice(c * FF_CHUNK, (c + 1) * FF_CHUNK)
        g = _dot(xn, w1_ref[:, sl])
        u = _dot(xn, w3_ref[:, sl])
        act_ref[:, sl] = (g * jax.nn.sigmoid(g) * u).astype(BF16)
    return _dot(act_ref[...], w2_ref[...])


def _ffn1_kernel(x_ref, norm_ref, w1_ref, w3_ref, w2_ref, o_ref, act_ref):
    x = x_ref[...]
    xn = _rms(x, norm_ref[...]).astype(BF16)
    o_ref[...] = x + FFN_RES * _swiglu(xn, w1_ref, w3_ref, w2_ref, act_ref)


def _ffn2_kernel(h_ref, p_ref, norm_ref, w1_ref, w3_ref, w2_ref, gnorm_ref, wg_ref, wp_ref,
                 pnorm_ref, fnorm_ref, o_ref, act_ref):
    h = h_ref[...]
    xn = _rms(h, norm_ref[...]).astype(BF16)
    h = h + FFN_RES * _swiglu(xn, w1_ref, w3_ref, w2_ref, act_ref)
    for hh in range(2):
        rs = slice(hh * (TM // 2), (hh + 1) * (TM // 2))
        hr = h[rs, :]
        gate = jax.nn.sigmoid(_dot(_rms(hr, gnorm_ref[...]).astype(BF16), wg_ref[...]))
        emb = _rms(_dot(p_ref[rs, :].astype(BF16), wp_ref[...]), pnorm_ref[...])
        o_ref[rs, :] = _rms(hr + gate * emb, fnorm_ref[...])


def _block_transpose(xs, lane_blk):
    xs = list(xs)
    for s in (4, 2, 1):
        upper = (lane_blk & s) != 0
        nxt = list(xs)
        for r in range(len(xs)):
            if r & s:
                continue
            a, b = xs[r], xs[r + s]
            nxt[r] = jnp.where(upper, pltpu.roll(b, SSM_GROUP * s, 1), a)
            nxt[r + s] = jnp.where(upper, b, pltpu.roll(a, LANES - SSM_GROUP * s, 1))
        xs = nxt
    return xs


def _mixer_kernel(h_ref, mnorm_ref, win_ref, btsup_ref, csup_ref, a8_ref,
                  dskip_ref, wglu_ref, bglu_ref, wpool_ref, pscale_ref, snorm_ref, pnorm_ref,
                  wout_ref, o_ref,
                  state_ref, zs_ref, ys_ref, vbuf_ref, s2_ref, s4_ref, s8_ref):
    l = pl.program_id(0)
    nb = h_ref.shape[0]
    nr = TT + POOL_TAIL
    rows = nb * KROWS

    @pl.when(l == 0)
    def _():
        state_ref[...] = jnp.zeros_like(state_ref)
        vbuf_ref[:, 0:POOL_TAIL, :] = jnp.zeros((nb, POOL_TAIL, POOL_WIDTH), F32)

    @pl.when(l > 0)
    def _():
        vbuf_ref[:, 0:POOL_TAIL, :] = vbuf_ref[:, TT:nr, :]

    h = h_ref[...].reshape(nb * TT, D_MODEL)
    z = _dot(_rms(h, mnorm_ref[...]).astype(BF16), win_ref[...])

    for b in range(nb):
        for v in range(SSM_VREGS):
            zs_ref[v, b * PITCH:b * PITCH + TT, :] = z[b * TT:(b + 1) * TT, v * LANES:(v + 1) * LANES]
        vbuf_ref[b, POOL_TAIL:nr, :] = z[b * TT:(b + 1) * TT, SSM_WIDTH:]

    lane_blk = lax.broadcasted_iota(jnp.int32, (rows, LANES), 1) // SSM_GROUP
    for v in range(SSM_VREGS):
        toks = [jnp.concatenate([zs_ref[v, pl.ds(FOLD * k + j, nb, stride=PITCH), :]
                                 for k in range(KROWS)], axis=0) for j in range(FOLD)]
        folded = _block_transpose(toks, lane_blk)
        ysup = []
        for pp in range(FOLD // 2):
            pair = v * (FOLD // 2) + pp
            base = pair * PAIR_W
            usup = jnp.concatenate([folded[2 * pp], folded[2 * pp + 1]], axis=1).astype(BF16)
            xy = _dot(usup, btsup_ref[pair])
            ar = jnp.broadcast_to(a8_ref[0:1, base:base + LANES], (nb, LANES))
            ai = jnp.broadcast_to(a8_ref[0:1, base + LANES:base + PAIR_W], (nb, LANES))
            sr = state_ref[:, base:base + LANES]
            si = state_ref[:, base + LANES:base + PAIR_W]
            prev_r, prev_i = [], []
            for k in range(KROWS):
                prev_r.append(sr)
                prev_i.append(si)
                xr = xy[k * nb:(k + 1) * nb, 0:LANES]
                xi = xy[k * nb:(k + 1) * nb, LANES:PAIR_W]
                sr, si = ar * sr - ai * si + xr, ar * si + ai * sr + xi
            state_ref[:, base:base + LANES] = sr
            state_ref[:, base + LANES:base + PAIR_W] = si
            sprev = jnp.concatenate([jnp.concatenate(prev_r, axis=0),
                                     jnp.concatenate(prev_i, axis=0)], axis=1).astype(BF16)
            y = xy[:, PAIR_W:] + _dot(sprev, csup_ref[pair])
            ysup.append(y[:, :LANES])
            ysup.append(y[:, LANES:])
        outs = _block_transpose(ysup, lane_blk)
        for j in range(FOLD):
            for k in range(KROWS):
                ys_ref[v, pl.ds(FOLD * k + j, nb, stride=PITCH), :] = outs[j][k * nb:(k + 1) * nb, :]

    y = jnp.concatenate(
        [jnp.concatenate([ys_ref[v, b * PITCH:b * PITCH + TT, :] for v in range(SSM_VREGS)], axis=1)
         for b in range(nb)], axis=0)
    y = y + dskip_ref[...] * z[:, :SSM_WIDTH]
    y = jax.nn.gelu(y, approximate=True)
    y = y * jax.nn.sigmoid(_dot(y.astype(BF16), wglu_ref[...]) + bglu_ref[...])
    yn_ssm = _rms(y, snorm_ref[...]).astype(BF16)

    s2_ref[:, 8:nr, :] = vbuf_ref[:, 8:nr, :] + vbuf_ref[:, 7:nr - 1, :]
    s4_ref[:, 16:nr, :] = s2_ref[:, 16:nr, LANES:] + s2_ref[:, 14:nr - 2, LANES:]
    s8_ref[:, 24:nr, :] = s4_ref[:, 24:nr, LANES:] + s4_ref[:, 20:nr - 4, LANES:]
    s16 = s8_ref[:, 32:nr, LANES:] + s8_ref[:, 24:nr - 8, LANES:]
    wins = [s2_ref[:, POOL_TAIL:nr, 0:LANES], s4_ref[:, POOL_TAIL:nr, 0:LANES],
            s8_ref[:, POOL_TAIL:nr, 0:LANES], s16]
    t = l * TT + lax.broadcasted_iota(jnp.int32, (1, TT, 1), 1)
    pooled = []
    for gi, w in enumerate(POOL_WINDOWS):
        inv_count = 1.0 / jnp.minimum(t + 1, w).astype(F32)
        vg = vbuf_ref[:, POOL_TAIL:nr, gi * LANES:(gi + 1) * LANES]
        pg = (wins[gi] * inv_count - vg).reshape(nb * TT, LANES).astype(BF16)
        pooled.append(_dot(pg, wpool_ref[gi]))
    y_pool = jnp.concatenate(pooled, axis=1) * pscale_ref[...]
    yn_pool = _rms(y_pool, pnorm_ref[...]).astype(BF16)

    out = h + _dot(yn_ssm, wout_ref[0:SSM_WIDTH, :]) + _dot(yn_pool, wout_ref[SSM_WIDTH:, :])
    o_ref[...] = out.reshape(nb, TT, D_MODEL)


def _pair_lanes(re, im):
    lead = re.shape[:-2]
    re = re.reshape(lead + (PAIRS, 2 * SSM_STATE))
    im = im.reshape(lead + (PAIRS, 2 * SSM_STATE))
    return jnp.concatenate([re, im], axis=-1).reshape(lead + (STATE_W,))


def _ssm_tables(a_re, a_im, log_dt, b_re, b_im, c_re, c_im):
    hi = lax.Precision.HIGHEST
    dt = jnp.exp(log_dt)[:, None]

    def apow(m):
        m = jnp.asarray(m, F32)[:, None, None]
        mag = jnp.exp(a_re * dt * m)
        return mag * jnp.cos(a_im * dt * m), mag * jnp.sin(a_im * dt * m)

    ar, ai = apow([1.0])
    ar, ai = ar[0], ai[0]
    den = a_re * a_re + a_im * a_im
    nr = ar - 1.0
    f_r = (nr * a_re + ai * a_im) / den
    f_i = (ai * a_re - nr * a_im) / den
    bb_r = f_r[..., None] * b_re - f_i[..., None] * b_im
    bb_i = f_r[..., None] * b_im + f_i[..., None] * b_re

    qr, qi = apow(np.arange(FOLD - 1, -1, -1))
    qr = jnp.transpose(qr, (1, 0, 2))[:, :, None, :]
    qi = jnp.transpose(qi, (1, 0, 2))[:, :, None, :]
    bt_r = jnp.transpose(bb_r, (0, 2, 1))[:, None]
    bt_i = jnp.transpose(bb_i, (0, 2, 1))[:, None]
    bs_r = (qr * bt_r - qi * bt_i).reshape(PAIRS, 2, LANES, SSM_STATE)
    bs_i = (qr * bt_i + qi * bt_r).reshape(PAIRS, 2, LANES, SSM_STATE)
    zb = jnp.zeros((PAIRS, LANES, SSM_STATE), F32)
    bsup = jnp.concatenate([
        jnp.concatenate([bs_r[:, 0], zb, bs_i[:, 0], zb], axis=-1),
        jnp.concatenate([zb, bs_r[:, 1], zb, bs_i[:, 1]], axis=-1)], axis=1)

    er, ei = apow(np.arange(1, FOLD + 1))
    er = jnp.transpose(er, (1, 0, 2))[:, :, None, :]
    ei = jnp.transpose(ei, (1, 0, 2))[:, :, None, :]
    m_r = (c_re[:, None] * er - c_im[:, None] * ei).reshape(SSM_GROUPS, LANES, SSM_STATE)
    m_i = (c_re[:, None] * ei + c_im[:, None] * er).reshape(SSM_GROUPS, LANES, SSM_STATE)
    ct_r = jnp.transpose(m_r, (0, 2, 1)).reshape(PAIRS, 2, SSM_STATE, LANES)
    ct_i = -jnp.transpose(m_i, (0, 2, 1)).reshape(PAIRS, 2, SSM_STATE, LANES)
    zc = jnp.zeros((PAIRS, SSM_STATE, LANES), F32)
    csup = jnp.concatenate([
        jnp.concatenate([ct_r[:, 0], zc], axis=-1),
        jnp.concatenate([zc, ct_r[:, 1]], axis=-1),
        jnp.concatenate([ct_i[:, 0], zc], axis=-1),
        jnp.concatenate([zc, ct_i[:, 1]], axis=-1)], axis=1)

    kr, ki = apow(np.arange(FOLD))
    kr = jnp.transpose(kr, (1, 0, 2))[:, :, None, :]
    ki = jnp.transpose(ki, (1, 0, 2))[:, :, None, :]
    w_r = c_re[:, None] * kr - c_im[:, None] * ki
    w_i = c_re[:, None] * ki + c_im[:, None] * kr
    kern = (jnp.einsum('gtpn,gnq->gtpq', w_r, bb_r, precision=hi)
            - jnp.einsum('gtpn,gnq->gtpq', w_i, bb_i, precision=hi))
    lag = np.arange(FOLD)[None, :] - np.arange(FOLD)[:, None]
    toe = jnp.take(kern, jnp.asarray(np.maximum(lag, 0).reshape(-1)), axis=1)
    toe = toe.reshape(SSM_GROUPS, FOLD, FOLD, SSM_GROUP, SSM_GROUP)
    toe = jnp.where(jnp.asarray(lag >= 0)[None, :, :, None, None], toe, 0.0)
    toe = jnp.transpose(toe, (0, 1, 4, 2, 3)).reshape(PAIRS, 2, LANES, LANES)
    zt = jnp.zeros((PAIRS, LANES, LANES), F32)
    tsup = jnp.concatenate([
        jnp.concatenate([toe[:, 0], zt], axis=-1),
        jnp.concatenate([zt, toe[:, 1]], axis=-1)], axis=1)

    btsup = jnp.concatenate([bsup, tsup], axis=-1)
    a8 = _pair_lanes(*apow([float(FOLD)]))
    return btsup.astype(BF16), csup.astype(BF16), a8


def _const_spec(shape):
    nd = len(shape)
    return pl.BlockSpec(shape, lambda *_: (0,) * nd)


def _params():
    return pltpu.CompilerParams(dimension_semantics=("arbitrary",), vmem_limit_bytes=VMEM_LIMIT)


def kernel(x, p, ffn1_norm, ffn1_w1, ffn1_w3, ffn1_w2, mix_norm, w_in, a_re, a_im, log_dt, b_re, b_im,
           c_re, c_im, d_skip, w_glu, b_glu, w_pool, pool_scale, ssm_out_norm, pool_out_norm, w_out,
           ffn2_norm, ffn2_w1, ffn2_w3, ffn2_w2, ple_gate_norm, w_ple_gate, w_ple_proj, ple_norm,
           final_norm):
    bsz, seqlen, _ = x.shape
    depth = ffn1_w1.shape[0]
    n_tok = bsz * seqlen
    assert n_tok % TM == 0 and seqlen % TT == 0 and bsz == SUBLANES

    row = lambda a: a.reshape(1, -1).astype(F32)
    tok_spec = pl.BlockSpec((TM, D_MODEL), lambda t: (t, 0))
    seq_spec = pl.BlockSpec((bsz, TT, D_MODEL), lambda l: (0, l, 0))

    assert depth == 1
    i = 0

    h = pl.pallas_call(
        _ffn1_kernel,
        grid=(n_tok // TM,),
        in_specs=[tok_spec, _const_spec((1, D_MODEL)), _const_spec((D_MODEL, D_FF)),
                  _const_spec((D_MODEL, D_FF)), _const_spec((D_FF, D_MODEL))],
        out_specs=tok_spec,
        out_shape=jax.ShapeDtypeStruct((n_tok, D_MODEL), F32),
        scratch_shapes=[pltpu.VMEM((TM, D_FF), BF16)],
        compiler_params=_params(),
        name="ffn1",
    )(x.reshape(n_tok, D_MODEL), row(ffn1_norm[i]), ffn1_w1[i].astype(BF16),
      ffn1_w3[i].astype(BF16), ffn1_w2[i].astype(BF16))

    btsup, csup, a8 = _ssm_tables(a_re[i], a_im[i], log_dt[i], b_re[i], b_im[i], c_re[i], c_im[i])
    consts = [row(mix_norm[i]), w_in[i].astype(BF16), btsup, csup, a8,
              row(d_skip[i]), w_glu[i].astype(BF16), row(b_glu[i]), w_pool[i].astype(BF16),
              row(pool_scale[i]), row(ssm_out_norm[i]), row(pool_out_norm[i]),
              w_out[i].astype(BF16)]
    nr = TT + POOL_TAIL
    h = pl.pallas_call(
        _mixer_kernel,
        grid=(seqlen // TT,),
        in_specs=[seq_spec] + [_const_spec(c.shape) for c in consts],
        out_specs=seq_spec,
        out_shape=jax.ShapeDtypeStruct((bsz, seqlen, D_MODEL), F32),
        scratch_shapes=[pltpu.VMEM((bsz, STATE_W), F32),
                        pltpu.VMEM((SSM_VREGS, bsz * PITCH, LANES), F32),
                        pltpu.VMEM((SSM_VREGS, bsz * PITCH, LANES), F32),
                        pltpu.VMEM((bsz, nr, POOL_WIDTH), F32),
                        pltpu.VMEM((bsz, nr, POOL_WIDTH), F32),
                        pltpu.VMEM((bsz, nr, POOL_WIDTH - LANES), F32),
                        pltpu.VMEM((bsz, nr, POOL_WIDTH - 2 * LANES), F32)],
        compiler_params=_params(),
        name="mixer",
    )(h.reshape(bsz, seqlen, D_MODEL), *consts)

    consts = [row(ffn2_norm[i]), ffn2_w1[i].astype(BF16), ffn2_w3[i].astype(BF16),
              ffn2_w2[i].astype(BF16), row(ple_gate_norm[i]), w_ple_gate[i].astype(BF16),
              w_ple_proj[i].astype(BF16), row(ple_norm[i]), row(final_norm)]
    h = pl.pallas_call(
        _ffn2_kernel,
        grid=(n_tok // TM,),
        in_specs=[tok_spec, pl.BlockSpec((TM, PLE_DIM), lambda t: (t, 0))]
                 + [_const_spec(c.shape) for c in consts],
        out_specs=tok_spec,
        out_shape=jax.ShapeDtypeStruct((n_tok, D_MODEL), F32),
        scratch_shapes=[pltpu.VMEM((TM, D_FF), BF16)],
        compiler_params=_params(),
        name="ffn2",
    )(h.reshape(n_tok, D_MODEL), p[i].reshape(n_tok, PLE_DIM), *consts)
    return h.reshape(bsz, seqlen, D_MODEL)
```

```python
import math

import jax
import jax.numpy as jnp
import numpy as np
from jax import lax
from jax.experimental import pallas as pl
from jax.experimental.pallas import tpu as pltpu

F32 = jnp.float32
BF16 = jnp.bfloat16

D_MODEL = 1024
D_FF = 2816
PLE_DIM = 256
SSM_WIDTH = 512
POOL_WIDTH = 512
SSM_GROUP = 16
SSM_GROUPS = 32
SSM_STATE = 64
POOL_WINDOWS = (2, 4, 8, 16)
FFN_RES = 0.5
EPS = 1e-6
GELU_C0 = math.sqrt(2.0 / math.pi)
GELU_C1 = 0.044715 * GELU_C0

LANES = 128
SUBLANES = 8
FOLD = LANES // SSM_GROUP
PAIRS = SSM_GROUPS // 2
PAIR_W = 2 * LANES
STATE_W = PAIRS * PAIR_W
SSM_VREGS = SSM_WIDTH // LANES

TM = 1024
TT = 128
KROWS = TT // FOLD
PITCH = TT + SUBLANES
FF_CHUNK = 256
POOL_TAIL = 32
VMEM_LIMIT = 56 * 1024 * 1024


def _rms(x, g):
    return x * lax.rsqrt(jnp.mean(x * x, axis=-1, keepdims=True) + EPS) * g


def _dot(a, b):
    return jnp.dot(a, b, preferred_element_type=F32)


def _rstd(x):
    return lax.rsqrt(jnp.mean(x * x, axis=-1, keepdims=True) + EPS)


def _swiglu(x, w1_ref, w3_ref, w2_ref, act_ref):
    xb = x.astype(BF16)
    r = jnp.broadcast_to(_rstd(x), (x.shape[0], FF_CHUNK))
    for c in range(D_FF // FF_CHUNK):
        sl = slice(c * FF_CHUNK, (c + 1) * FF_CHUNK)
        g = _dot(xb, w1_ref[:, sl]) * r
        u = _dot(xb, w3_ref[:, sl]) * r
        act_ref[:, sl] = (g * jax.nn.sigmoid(g) * u).astype(BF16)
    return _dot(act_ref[...], w2_ref[...])


def _ffn1_kernel(x_ref, w1_ref, w3_ref, w2_ref, o_ref, act_ref):
    x = x_ref[...]
    o_ref[...] = x + FFN_RES * _swiglu(x, w1_ref, w3_ref, w2_ref, act_ref)


def _ffn2_kernel(h_ref, p_ref, w1_ref, w3_ref, w2_ref, wg_ref, wp_ref, pnorm_ref, fnorm_ref,
                 o_ref, act_ref):
    h = h_ref[...]
    h = h + FFN_RES * _swiglu(h, w1_ref, w3_ref, w2_ref, act_ref)
    for hh in range(2):
        rs = slice(hh * (TM // 2), (hh + 1) * (TM // 2))
        hr = h[rs, :]
        gate = jax.nn.sigmoid(_dot(hr.astype(BF16), wg_ref[...]) * _rstd(hr))
        emb = _rms(_dot(p_ref[rs, :].astype(BF16), wp_ref[...]), pnorm_ref[...])
        o_ref[rs, :] = _rms(hr + gate * emb, fnorm_ref[...])


def _block_transpose(xs, lane_blk):
    xs = list(xs)
    for s in (4, 2, 1):
        upper = (lane_blk & s) != 0
        nxt = list(xs)
        for r in range(len(xs)):
            if r & s:
                continue
            a, b = xs[r], xs[r + s]
            nxt[r] = jnp.where(upper, pltpu.roll(b, SSM_GROUP * s, 1), a)
            nxt[r + s] = jnp.where(upper, b, pltpu.roll(a, LANES - SSM_GROUP * s, 1))
        xs = nxt
    return xs


def _mixer_kernel(h_ref, win_ref, btsup_ref, csup_ref, a8_ref,
                  dskip_ref, wglu_ref, bglu_ref, wpool_ref, wout_ref, o_ref,
                  state_ref, zs_ref, ys_ref, vbuf_ref, s2_ref, s4_ref, s8_ref):
    l = pl.program_id(0)
    nb = h_ref.shape[0]
    nr = TT + POOL_TAIL
    rows = nb * KROWS

    @pl.when(l == 0)
    def _():
        state_ref[...] = jnp.zeros_like(state_ref)
        vbuf_ref[:, 0:POOL_TAIL, :] = jnp.zeros((nb, POOL_TAIL, POOL_WIDTH), F32)

    @pl.when(l > 0)
    def _():
        vbuf_ref[:, 0:POOL_TAIL, :] = vbuf_ref[:, TT:nr, :]

    h = h_ref[...].reshape(nb * TT, D_MODEL)
    z = _dot(h.astype(BF16), win_ref[...]) * _rstd(h)

    for b in range(nb):
        for v in range(SSM_VREGS):
            zs_ref[v, b * PITCH:b * PITCH + TT, :] = z[b * TT:(b + 1) * TT, v * LANES:(v + 1) * LANES]
        vbuf_ref[b, POOL_TAIL:nr, :] = z[b * TT:(b + 1) * TT, SSM_WIDTH:]

    lane_blk = lax.broadcasted_iota(jnp.int32, (rows, LANES), 1) // SSM_GROUP
    for v in range(SSM_VREGS):
        toks = [jnp.concatenate([zs_ref[v, pl.ds(FOLD * k + j, nb, stride=PITCH), :]
                                 for k in range(KROWS)], axis=0) for j in range(FOLD)]
        folded = _block_transpose(toks, lane_blk)
        ysup = []
        for pp in range(FOLD // 2):
            pair = v * (FOLD // 2) + pp
            base = pair * PAIR_W
            usup = jnp.concatenate([folded[2 * pp], folded[2 * pp + 1]], axis=1).astype(BF16)
            xy = _dot(usup, btsup_ref[pair])
            ar = jnp.broadcast_to(a8_ref[0:1, base:base + LANES], (nb, LANES))
            ai = jnp.broadcast_to(a8_ref[0:1, base + LANES:base + PAIR_W], (nb, LANES))
            sr = state_ref[:, base:base + LANES]
            si = state_ref[:, base + LANES:base + PAIR_W]
            prev_r, prev_i = [], []
            for k in range(KROWS):
                prev_r.append(sr)
                prev_i.append(si)
                xr = xy[k * nb:(k + 1) * nb, 0:LANES]
                xi = xy[k * nb:(k + 1) * nb, LANES:PAIR_W]
                sr, si = ar * sr - ai * si + xr, ar * si + ai * sr + xi
            state_ref[:, base:base + LANES] = sr
            state_ref[:, base + LANES:base + PAIR_W] = si
            sprev = jnp.concatenate([jnp.concatenate(prev_r, axis=0),
                                     jnp.concatenate(prev_i, axis=0)], axis=1).astype(BF16)
            y = xy[:, PAIR_W:] + _dot(sprev, csup_ref[pair])
            ysup.append(y[:, :LANES])
            ysup.append(y[:, LANES:])
        outs = _block_transpose(ysup, lane_blk)
        for j in range(FOLD):
            for k in range(KROWS):
                ys_ref[v, pl.ds(FOLD * k + j, nb, stride=PITCH), :] = outs[j][k * nb:(k + 1) * nb, :]

    y = jnp.concatenate(
        [jnp.concatenate([ys_ref[v, b * PITCH:b * PITCH + TT, :] for v in range(SSM_VREGS)], axis=1)
         for b in range(nb)], axis=0)
    y = y + dskip_ref[...] * z[:, :SSM_WIDTH]
    y = y * (0.5 + 0.5 * jnp.tanh(y * (GELU_C0 + GELU_C1 * (y * y))))
    y = y * jax.nn.sigmoid(_dot(y.astype(BF16), wglu_ref[...]) + bglu_ref[...])

    s2_ref[:, 8:nr, :] = vbuf_ref[:, 8:nr, :] + vbuf_ref[:, 7:nr - 1, :]
    s4_ref[:, 16:nr, :] = s2_ref[:, 16:nr, LANES:] + s2_ref[:, 14:nr - 2, LANES:]
    s8_ref[:, 24:nr, :] = s4_ref[:, 24:nr, LANES:] + s4_ref[:, 20:nr - 4, LANES:]
    s16 = s8_ref[:, 32:nr, LANES:] + s8_ref[:, 24:nr - 8, LANES:]
    wins = [s2_ref[:, POOL_TAIL:nr, 0:LANES], s4_ref[:, POOL_TAIL:nr, 0:LANES],
            s8_ref[:, POOL_TAIL:nr, 0:LANES], s16]
    t = l * TT + lax.broadcasted_iota(jnp.int32, (1, TT, 1), 1)
    pooled = []
    for gi, w in enumerate(POOL_WINDOWS):
        inv_count = 1.0 / jnp.minimum(t + 1, w).astype(F32)
        vg = vbuf_ref[:, POOL_TAIL:nr, gi * LANES:(gi + 1) * LANES]
        pg = (wins[gi] * inv_count - vg).reshape(nb * TT, LANES).astype(BF16)
        pooled.append(_dot(pg, wpool_ref[gi]))
    y_pool = jnp.concatenate(pooled, axis=1)

    out = (h + _dot(y.astype(BF16), wout_ref[0:SSM_WIDTH, :]) * _rstd(y)
           + _dot(y_pool.astype(BF16), wout_ref[SSM_WIDTH:, :]) * _rstd(y_pool))
    o_ref[...] = out.reshape(nb, TT, D_MODEL)


def _pair_lanes(re, im):
    lead = re.shape[:-2]
    re = re.reshape(lead + (PAIRS, 2 * SSM_STATE))
    im = im.reshape(lead + (PAIRS, 2 * SSM_STATE))
    return jnp.concatenate([re, im], axis=-1).reshape(lead + (STATE_W,))


def _ssm_tables(a_re, a_im, log_dt, b_re, b_im, c_re, c_im):
    hi = lax.Precision.HIGHEST
    dt = jnp.exp(log_dt)[:, None]

    def apow(m):
        m = jnp.asarray(m, F32)[:, None, None]
        mag = jnp.exp(a_re * dt * m)
        return mag * jnp.cos(a_im * dt * m), mag * jnp.sin(a_im * dt * m)

    ar, ai = apow([1.0])
    ar, ai = ar[0], ai[0]
    den = a_re * a_re + a_im * a_im
    nr = ar - 1.0
    f_r = (nr * a_re + ai * a_im) / den
    f_i = (ai * a_re - nr * a_im) / den
    bb_r = f_r[..., None] * b_re - f_i[..., None] * b_im
    bb_i = f_r[..., None] * b_im + f_i[..., None] * b_re

    qr, qi = apow(np.arange(FOLD - 1, -1, -1))
    qr = jnp.transpose(qr, (1, 0, 2))[:, :, None, :]
    qi = jnp.transpose(qi, (1, 0, 2))[:, :, None, :]
    bt_r = jnp.transpose(bb_r, (0, 2, 1))[:, None]
    bt_i = jnp.transpose(bb_i, (0, 2, 1))[:, None]
    bs_r = (qr * bt_r - qi * bt_i).reshape(PAIRS, 2, LANES, SSM_STATE)
    bs_i = (qr * bt_i + qi * bt_r).reshape(PAIRS, 2, LANES, SSM_STATE)
    zb = jnp.zeros((PAIRS, LANES, SSM_STATE), F32)
    bsup = jnp.concatenate([
        jnp.concatenate([bs_r[:, 0], zb, bs_i[:, 0], zb], axis=-1),
        jnp.concatenate([zb, bs_r[:, 1], zb, bs_i[:, 1]], axis=-1)], axis=1)

    er, ei = apow(np.arange(1, FOLD + 1))
    er = jnp.transpose(er, (1, 0, 2))[:, :, None, :]
    ei = jnp.transpose(ei, (1, 0, 2))[:, :, None, :]
    m_r = (c_re[:, None] * er - c_im[:, None] * ei).reshape(SSM_GROUPS, LANES, SSM_STATE)
    m_i = (c_re[:, None] * ei + c_im[:, None] * er).reshape(SSM_GROUPS, LANES, SSM_STATE)
    ct_r = jnp.transpose(m_r, (0, 2, 1)).reshape(PAIRS, 2, SSM_STATE, LANES)
    ct_i = -jnp.transpose(m_i, (0, 2, 1)).reshape(PAIRS, 2, SSM_STATE, LANES)
    zc = jnp.zeros((PAIRS, SSM_STATE, LANES), F32)
    csup = jnp.concatenate([
        jnp.concatenate([ct_r[:, 0], zc], axis=-1),
        jnp.concatenate([zc, ct_r[:, 1]], axis=-1),
        jnp.concatenate([ct_i[:, 0], zc], axis=-1),
        jnp.concatenate([zc, ct_i[:, 1]], axis=-1)], axis=1)

    kr, ki = apow(np.arange(FOLD))
    kr = jnp.transpose(kr, (1, 0, 2))[:, :, None, :]
    ki = jnp.transpose(ki, (1, 0, 2))[:, :, None, :]
    w_r = c_re[:, None] * kr - c_im[:, None] * ki
    w_i = c_re[:, None] * ki + c_im[:, None] * kr
    kern = (jnp.einsum('gtpn,gnq->gtpq', w_r, bb_r, precision=hi)
            - jnp.einsum('gtpn,gnq->gtpq', w_i, bb_i, precision=hi))
    lag = np.arange(FOLD)[None, :] - np.arange(FOLD)[:, None]
    toe = jnp.take(kern, jnp.asarray(np.maximum(lag, 0).reshape(-1)), axis=1)
    toe = toe.reshape(SSM_GROUPS, FOLD, FOLD, SSM_GROUP, SSM_GROUP)
    toe = jnp.where(jnp.asarray(lag >= 0)[None, :, :, None, None], toe, 0.0)
    toe = jnp.transpose(toe, (0, 1, 4, 2, 3)).reshape(PAIRS, 2, LANES, LANES)
    zt = jnp.zeros((PAIRS, LANES, LANES), F32)
    tsup = jnp.concatenate([
        jnp.concatenate([toe[:, 0], zt], axis=-1),
        jnp.concatenate([zt, toe[:, 1]], axis=-1)], axis=1)

    btsup = jnp.concatenate([bsup, tsup], axis=-1)
    a8 = _pair_lanes(*apow([float(FOLD)]))
    return btsup.astype(BF16), csup.astype(BF16), a8


def _const_spec(shape):
    nd = len(shape)
    return pl.BlockSpec(shape, lambda *_: (0,) * nd)


def _params():
    return pltpu.CompilerParams(dimension_semantics=("arbitrary",), vmem_limit_bytes=VMEM_LIMIT)


def kernel(x, p, ffn1_norm, ffn1_w1, ffn1_w3, ffn1_w2, mix_norm, w_in, a_re, a_im, log_dt, b_re, b_im,
           c_re, c_im, d_skip, w_glu, b_glu, w_pool, pool_scale, ssm_out_norm, pool_out_norm, w_out,
           ffn2_norm, ffn2_w1, ffn2_w3, ffn2_w2, ple_gate_norm, w_ple_gate, w_ple_proj, ple_norm,
           final_norm):
    bsz, seqlen, _ = x.shape
    depth = ffn1_w1.shape[0]
    n_tok = bsz * seqlen
    assert n_tok % TM == 0 and seqlen % TT == 0 and bsz == SUBLANES

    row = lambda a: a.reshape(1, -1).astype(F32)
    gained = lambda w, g: (g.astype(F32)[:, None] * w.astype(F32)).astype(BF16)
    tok_spec = pl.BlockSpec((TM, D_MODEL), lambda t: (t, 0))
    seq_spec = pl.BlockSpec((bsz, TT, D_MODEL), lambda l: (0, l, 0))

    assert depth == 1
    i = 0

    h = pl.pallas_call(
        _ffn1_kernel,
        grid=(n_tok // TM,),
        in_specs=[tok_spec, _const_spec((D_MODEL, D_FF)), _const_spec((D_MODEL, D_FF)),
                  _const_spec((D_FF, D_MODEL))],
        out_specs=tok_spec,
        out_shape=jax.ShapeDtypeStruct((n_tok, D_MODEL), F32),
        scratch_shapes=[pltpu.VMEM((TM, D_FF), BF16)],
        compiler_params=_params(),
        name="ffn1",
    )(x.reshape(n_tok, D_MODEL), gained(ffn1_w1[i], ffn1_norm[i]), gained(ffn1_w3[i], ffn1_norm[i]),
      ffn1_w2[i].astype(BF16))

    btsup, csup, a8 = _ssm_tables(a_re[i], a_im[i], log_dt[i], b_re[i], b_im[i], c_re[i], c_im[i])
    out_gain = jnp.concatenate([ssm_out_norm[i], pool_out_norm[i]])
    pool_cols = pool_scale[i].astype(F32).reshape(len(POOL_WINDOWS), 1, LANES)
    consts = [gained(w_in[i], mix_norm[i]), btsup, csup, a8,
              row(d_skip[i]), w_glu[i].astype(BF16), row(b_glu[i]),
              (w_pool[i].astype(F32) * pool_cols).astype(BF16), gained(w_out[i], out_gain)]
    nr = TT + POOL_TAIL
    h = pl.pallas_call(
        _mixer_kernel,
        grid=(seqlen // TT,),
        in_specs=[seq_spec] + [_const_spec(c.shape) for c in consts],
        out_specs=seq_spec,
        out_shape=jax.ShapeDtypeStruct((bsz, seqlen, D_MODEL), F32),
        scratch_shapes=[pltpu.VMEM((bsz, STATE_W), F32),
                        pltpu.VMEM((SSM_VREGS, bsz * PITCH, LANES), F32),
                        pltpu.VMEM((SSM_VREGS, bsz * PITCH, LANES), F32),
                        pltpu.VMEM((bsz, nr, POOL_WIDTH), F32),
                        pltpu.VMEM((bsz, nr, POOL_WIDTH), F32),
                        pltpu.VMEM((bsz, nr, POOL_WIDTH - LANES), F32),
                        pltpu.VMEM((bsz, nr, POOL_WIDTH - 2 * LANES), F32)],
        compiler_params=_params(),
        name="mixer",
    )(h.reshape(bsz, seqlen, D_MODEL), *consts)

    consts = [gained(ffn2_w1[i], ffn2_norm[i]), gained(ffn2_w3[i], ffn2_norm[i]),
              ffn2_w2[i].astype(BF16), gained(w_ple_gate[i], ple_gate_norm[i]),
              w_ple_proj[i].astype(BF16), row(ple_norm[i]), row(final_norm)]
    h = pl.pallas_call(
        _ffn2_kernel,
        grid=(n_tok // TM,),
        in_specs=[tok_spec, pl.BlockSpec((TM, PLE_DIM), lambda t: (t, 0))]
                 + [_const_spec(c.shape) for c in consts],
        out_specs=tok_spec,
        out_shape=jax.ShapeDtypeStruct((n_tok, D_MODEL), F32),
        scratch_shapes=[pltpu.VMEM((TM, D_FF), BF16)],
        compiler_params=_params(),
        name="ffn2",
    )(h.reshape(n_tok, D_MODEL), p[i].reshape(n_tok, PLE_DIM), *consts)
    return h.reshape(bsz, seqlen, D_MODEL)
```

```python
import math

import jax
import jax.numpy as jnp
import numpy as np
from jax import lax
from jax.experimental import pallas as pl
from jax.experimental.pallas import tpu as pltpu

F32 = jnp.float32
BF16 = jnp.bfloat16

D_MODEL = 1024
D_FF = 2816
PLE_DIM = 256
SSM_WIDTH = 512
POOL_WIDTH = 512
SSM_GROUP = 16
SSM_GROUPS = 32
SSM_STATE = 64
POOL_WINDOWS = (2, 4, 8, 16)
FFN_RES = 0.5
EPS = 1e-6
GELU_C0 = math.sqrt(2.0 / math.pi)
GELU_C1 = 0.044715 * GELU_C0

LANES = 128
SUBLANES = 8
FOLD = LANES // SSM_GROUP
PAIRS = SSM_GROUPS // 2
PAIR_W = 2 * LANES
STATE_W = PAIRS * PAIR_W
SSM_VREGS = SSM_WIDTH // LANES

TM = 1024
TT = 128
KROWS = TT // FOLD
PITCH = TT + SUBLANES
FF_CHUNK = 256
POOL_TAIL = 32
VMEM_LIMIT = 56 * 1024 * 1024


def _rms(x, g):
    return x * lax.rsqrt(jnp.mean(x * x, axis=-1, keepdims=True) + EPS) * g


def _dot(a, b):
    return jnp.dot(a, b, preferred_element_type=F32)


def _rstd(x):
    return lax.rsqrt(jnp.mean(x * x, axis=-1, keepdims=True) + EPS)


def _swiglu(x, w1_ref, w3_ref, w2_ref, act_ref):
    xb = x.astype(BF16)
    r = jnp.broadcast_to(_rstd(x), (x.shape[0], FF_CHUNK))
    for c in range(D_FF // FF_CHUNK):
        sl = slice(c * FF_CHUNK, (c + 1) * FF_CHUNK)
        g = _dot(xb, w1_ref[:, sl]) * r
        u = _dot(xb, w3_ref[:, sl]) * r
        act_ref[:, sl] = (g * jax.nn.sigmoid(g) * u).astype(BF16)
    return _dot(act_ref[...], w2_ref[...])


def _ffn1_kernel(x_ref, w1_ref, w3_ref, w2_ref, o_ref, act_ref):
    x = x_ref[...]
    o_ref[...] = x + FFN_RES * _swiglu(x, w1_ref, w3_ref, w2_ref, act_ref)


def _ffn2_kernel(h_ref, p_ref, w1_ref, w3_ref, w2_ref, wg_ref, wp_ref, pnorm_ref, fnorm_ref,
                 o_ref, act_ref):
    h = h_ref[...]
    h = h + FFN_RES * _swiglu(h, w1_ref, w3_ref, w2_ref, act_ref)
    for hh in range(2):
        rs = slice(hh * (TM // 2), (hh + 1) * (TM // 2))
        hr = h[rs, :]
        gate = jax.nn.sigmoid(_dot(hr.astype(BF16), wg_ref[...]) * _rstd(hr))
        emb = _rms(_dot(p_ref[rs, :].astype(BF16), wp_ref[...]), pnorm_ref[...])
        o_ref[rs, :] = _rms(hr + gate * emb, fnorm_ref[...])


def _block_transpose(xs, lane_blk):
    xs = list(xs)
    for s in (4, 2, 1):
        upper = (lane_blk & s) != 0
        nxt = list(xs)
        for r in range(len(xs)):
            if r & s:
                continue
            a, b = xs[r], xs[r + s]
            nxt[r] = jnp.where(upper, pltpu.roll(b, SSM_GROUP * s, 1), a)
            nxt[r + s] = jnp.where(upper, b, pltpu.roll(a, LANES - SSM_GROUP * s, 1))
        xs = nxt
    return xs


def _mixer_kernel(h_ref, win_ref, btsup_ref, csup_ref, a8_ref,
                  dskip_ref, wglu_ref, bglu_ref, wpool_ref, wout_ref, o_ref,
                  state_ref, zs_ref, ys_ref, vbuf_ref, s2_ref, s4_ref, s8_ref):
    l = pl.program_id(0)
    nb = h_ref.shape[0]
    nr = TT + POOL_TAIL
    rows = nb * KROWS

    @pl.when(l == 0)
    def _():
        state_ref[...] = jnp.zeros_like(state_ref)
        vbuf_ref[:, 0:POOL_TAIL, :] = jnp.zeros((nb, POOL_TAIL, POOL_WIDTH), F32)

    @pl.when(l > 0)
    def _():
        vbuf_ref[:, 0:POOL_TAIL, :] = vbuf_ref[:, TT:nr, :]

    h = h_ref[...].reshape(nb * TT, D_MODEL)
    z = _dot(h.astype(BF16), win_ref[...]) * _rstd(h)

    for b in range(nb):
        for v in range(SSM_VREGS):
            zs_ref[v, b * PITCH:b * PITCH + TT, :] = z[b * TT:(b + 1) * TT, v * LANES:(v + 1) * LANES]
        vbuf_ref[b, POOL_TAIL:nr, :] = z[b * TT:(b + 1) * TT, SSM_WIDTH:]

    lane_blk = lax.broadcasted_iota(jnp.int32, (rows, LANES), 1) // SSM_GROUP
    for v in range(SSM_VREGS):
        toks = [jnp.concatenate([zs_ref[v, pl.ds(FOLD * k + j, nb, stride=PITCH), :]
                                 for k in range(KROWS)], axis=0) for j in range(FOLD)]
        folded = _block_transpose(toks, lane_blk)
        ysup = []
        for pp in range(FOLD // 2):
            pair = v * (FOLD // 2) + pp
            base = pair * PAIR_W
            usup = jnp.concatenate([folded[2 * pp], folded[2 * pp + 1]], axis=1).astype(BF16)
            xy = _dot(usup, btsup_ref[pair])
            ar = jnp.broadcast_to(a8_ref[0:1, base:base + LANES], (nb, LANES))
            ai = jnp.broadcast_to(a8_ref[0:1, base + LANES:base + PAIR_W], (nb, LANES))
            sr = state_ref[:, base:base + LANES]
            si = state_ref[:, base + LANES:base + PAIR_W]
            prev_r, prev_i = [], []
            for k in range(KROWS):
                prev_r.append(sr)
                prev_i.append(si)
                xr = xy[k * nb:(k + 1) * nb, 0:LANES]
                xi = xy[k * nb:(k + 1) * nb, LANES:PAIR_W]
                sr, si = ar * sr - ai * si + xr, ar * si + ai * sr + xi
            state_ref[:, base:base + LANES] = sr
            state_ref[:, base + LANES:base + PAIR_W] = si
            sprev = jnp.concatenate([jnp.concatenate(prev_r, axis=0),
                                     jnp.concatenate(prev_i, axis=0)], axis=1).astype(BF16)
            y = xy[:, PAIR_W:] + _dot(sprev, csup_ref[pair])
            ysup.append(y[:, :LANES])
            ysup.append(y[:, LANES:])
        outs = _block_transpose(ysup, lane_blk)
        for j in range(FOLD):
            for k in range(KROWS):
                ys_ref[v, pl.ds(FOLD * k + j, nb, stride=PITCH), :] = outs[j][k * nb:(k + 1) * nb, :]

    y = jnp.concatenate(
        [jnp.concatenate([ys_ref[v, b * PITCH:b * PITCH + TT, :] for v in range(SSM_VREGS)], axis=1)
         for b in range(nb)], axis=0)
    y = y + dskip_ref[...] * z[:, :SSM_WIDTH]
    y = y * (0.5 + 0.5 * jnp.tanh(y * (GELU_C0 + GELU_C1 * (y * y))))
    y = y * jax.nn.sigmoid(_dot(y.astype(BF16), wglu_ref[...]) + bglu_ref[...])

    s2_ref[:, 8:nr, :] = vbuf_ref[:, 8:nr, :] + vbuf_ref[:, 7:nr - 1, :]
    s4_ref[:, 16:nr, :] = s2_ref[:, 16:nr, LANES:] + s2_ref[:, 14:nr - 2, LANES:]
    s8_ref[:, 24:nr, :] = s4_ref[:, 24:nr, LANES:] + s4_ref[:, 20:nr - 4, LANES:]
    s16 = s8_ref[:, 32:nr, LANES:] + s8_ref[:, 24:nr - 8, LANES:]
    wins = [s2_ref[:, POOL_TAIL:nr, 0:LANES], s4_ref[:, POOL_TAIL:nr, 0:LANES],
            s8_ref[:, POOL_TAIL:nr, 0:LANES], s16]
    t = l * TT + lax.broadcasted_iota(jnp.int32, (1, TT, 1), 1)
    pooled = []
    for gi, w in enumerate(POOL_WINDOWS):
        inv_count = 1.0 / jnp.minimum(t + 1, w).astype(F32)
        vg = vbuf_ref[:, POOL_TAIL:nr, gi * LANES:(gi + 1) * LANES]
        pg = (wins[gi] * inv_count - vg).reshape(nb * TT, LANES).astype(BF16)
        pooled.append(_dot(pg, wpool_ref[gi]))
    y_pool = jnp.concatenate(pooled, axis=1)

    out = (h + _dot(y.astype(BF16), wout_ref[0:SSM_WIDTH, :]) * _rstd(y)
           + _dot(y_pool.astype(BF16), wout_ref[SSM_WIDTH:, :]) * _rstd(y_pool))
    o_ref[...] = out.reshape(nb, TT, D_MODEL)


def _pair_blocks(x):
    r = x.shape[1]
    eye = jnp.eye(2, dtype=F32).reshape(1, 2, 1, 2, 1)
    return (x.reshape(PAIRS, 2, r, 1, LANES) * eye).reshape(PAIRS, 2 * r, PAIR_W)


def _pair_lanes(re, im):
    lead = re.shape[:-2]
    re = re.reshape(lead + (PAIRS, 2 * SSM_STATE))
    im = im.reshape(lead + (PAIRS, 2 * SSM_STATE))
    return jnp.concatenate([re, im], axis=-1).reshape(lead + (STATE_W,))


def _ssm_tables(a_re, a_im, log_dt, b_re, b_im, c_re, c_im):
    hi = lax.Precision.HIGHEST
    g = SSM_GROUPS
    dt = jnp.exp(log_dt)[:, None]
    m = jnp.arange(FOLD + 1, dtype=F32)
    mag = jnp.exp((a_re * dt)[..., None] * m)
    ang = (a_im * dt)[..., None] * m
    pw = jnp.concatenate([mag * jnp.cos(ang), mag * jnp.sin(ang)], axis=0)
    ar, ai = pw[:g, :, 1], pw[g:, :, 1]
    den = a_re * a_re + a_im * a_im
    nr = ar - 1.0
    f_r = (nr * a_re + ai * a_im) / den
    f_i = (ai * a_re - nr * a_im) / den
    bb_r = f_r[..., None] * b_re - f_i[..., None] * b_im
    bb_i = f_r[..., None] * b_im + f_i[..., None] * b_re

    tok = np.arange(LANES) // SSM_GROUP
    chan = np.arange(LANES) % SSM_GROUP
    mm = np.arange(FOLD + 1)[:, None]
    sel = np.concatenate([mm == tok, mm == tok + 1, mm == FOLD - 1 - tok], axis=1).astype(np.float32)
    ohp = (np.arange(SSM_GROUP)[:, None] == chan).astype(np.float32)
    pe = jnp.einsum('gnm,ml->gnl', pw, sel, precision=hi)

    def powers(k):
        return pe[:g, :, k * LANES:(k + 1) * LANES], pe[g:, :, k * LANES:(k + 1) * LANES]

    def cmul(xr, xi, yr, yi):
        return xr * yr - xi * yi, xr * yi + xi * yr

    ct = jnp.einsum('gpn,pl->gnl', jnp.concatenate([c_re, c_im], axis=0), ohp, precision=hi)
    bt = jnp.einsum('gnp,pl->gnl', jnp.concatenate([bb_r, bb_i], axis=0), ohp, precision=hi)
    ct_r, ct_i, bt_r, bt_i = ct[:g], ct[g:], bt[:g], bt[g:]

    m_r, m_i = cmul(ct_r, ct_i, *powers(1))
    csup = jnp.concatenate([_pair_blocks(m_r), _pair_blocks(-m_i)], axis=1)

    s_r, s_i = cmul(bt_r, bt_i, *powers(2))
    bsup = jnp.swapaxes(jnp.concatenate([_pair_blocks(s_r), _pair_blocks(s_i)], axis=1), 1, 2)

    w_r, w_i = cmul(ct_r, ct_i, *powers(0))
    krow = jnp.einsum('gmq,gml->gql', jnp.concatenate([bb_r, -bb_i], axis=1),
                      jnp.concatenate([w_r, w_i], axis=1), precision=hi)
    kpad = jnp.pad(krow, ((0, 0), (0, 0), (LANES - SSM_GROUP, 0)))
    toe = jnp.stack([kpad[:, :, LANES - SSM_GROUP * (i + 1):2 * LANES - SSM_GROUP * (i + 1)]
                     for i in range(FOLD)], axis=1).reshape(g, LANES, LANES)

    btsup = jnp.concatenate([bsup, _pair_blocks(toe)], axis=-1)
    a8 = _pair_lanes(pw[:g, :, FOLD], pw[g:, :, FOLD]).reshape(1, STATE_W)
    return btsup.astype(BF16), csup.astype(BF16), a8


def _const_spec(shape):
    nd = len(shape)
    return pl.BlockSpec(shape, lambda *_: (0,) * nd)


def _params():
    return pltpu.CompilerParams(dimension_semantics=("arbitrary",), vmem_limit_bytes=VMEM_LIMIT)


def kernel(x, p, ffn1_norm, ffn1_w1, ffn1_w3, ffn1_w2, mix_norm, w_in, a_re, a_im, log_dt, b_re, b_im,
           c_re, c_im, d_skip, w_glu, b_glu, w_pool, pool_scale, ssm_out_norm, pool_out_norm, w_out,
           ffn2_norm, ffn2_w1, ffn2_w3, ffn2_w2, ple_gate_norm, w_ple_gate, w_ple_proj, ple_norm,
           final_norm):
    bsz, seqlen, _ = x.shape
    depth = ffn1_w1.shape[0]
    n_tok = bsz * seqlen
    assert n_tok % TM == 0 and seqlen % TT == 0 and bsz == SUBLANES

    row = lambda a: a.reshape(1, -1).astype(F32)
    gained = lambda w, g: (g.astype(F32)[:, None] * w.astype(F32)).astype(BF16)
    tok_spec = pl.BlockSpec((TM, D_MODEL), lambda t: (t, 0))
    seq_spec = pl.BlockSpec((bsz, TT, D_MODEL), lambda l: (0, l, 0))

    assert depth == 1
    i = 0

    h = pl.pallas_call(
        _ffn1_kernel,
        grid=(n_tok // TM,),
        in_specs=[tok_spec, _const_spec((D_MODEL, D_FF)), _const_spec((D_MODEL, D_FF)),
                  _const_spec((D_FF, D_MODEL))],
        out_specs=tok_spec,
        out_shape=jax.ShapeDtypeStruct((n_tok, D_MODEL), F32),
        scratch_shapes=[pltpu.VMEM((TM, D_FF), BF16)],
        compiler_params=_params(),
        name="ffn1",
    )(x.reshape(n_tok, D_MODEL), gained(ffn1_w1[i], ffn1_norm[i]), gained(ffn1_w3[i], ffn1_norm[i]),
      ffn1_w2[i].astype(BF16))

    btsup, csup, a8 = _ssm_tables(a_re[i], a_im[i], log_dt[i], b_re[i], b_im[i], c_re[i], c_im[i])
    out_gain = jnp.concatenate([ssm_out_norm[i], pool_out_norm[i]])
    pool_cols = pool_scale[i].astype(F32).reshape(len(POOL_WINDOWS), 1, LANES)
    consts = [gained(w_in[i], mix_norm[i]), btsup, csup, a8,
              row(d_skip[i]), w_glu[i].astype(BF16), row(b_glu[i]),
              (w_pool[i].astype(F32) * pool_cols).astype(BF16), gained(w_out[i], out_gain)]
    nr = TT + POOL_TAIL
    h = pl.pallas_call(
        _mixer_kernel,
        grid=(seqlen // TT,),
        in_specs=[seq_spec] + [_const_spec(c.shape) for c in consts],
        out_specs=seq_spec,
        out_shape=jax.ShapeDtypeStruct((bsz, seqlen, D_MODEL), F32),
        scratch_shapes=[pltpu.VMEM((bsz, STATE_W), F32),
                        pltpu.VMEM((SSM_VREGS, bsz * PITCH, LANES), F32),
                        pltpu.VMEM((SSM_VREGS, bsz * PITCH, LANES), F32),
                        pltpu.VMEM((bsz, nr, POOL_WIDTH), F32),
                        pltpu.VMEM((bsz, nr, POOL_WIDTH), F32),
                        pltpu.VMEM((bsz, nr, POOL_WIDTH - LANES), F32),
                        pltpu.VMEM((bsz, nr, POOL_WIDTH - 2 * LANES), F32)],
        compiler_params=_params(),
        name="mixer",
    )(h.reshape(bsz, seqlen, D_MODEL), *consts)

    consts = [gained(ffn2_w1[i], ffn2_norm[i]), gained(ffn2_w3[i], ffn2_norm[i]),
              ffn2_w2[i].astype(BF16), gained(w_ple_gate[i], ple_gate_norm[i]),
              w_ple_proj[i].astype(BF16), row(ple_norm[i]), row(final_norm)]
    h = pl.pallas_call(
        _ffn2_kernel,
        grid=(n_tok // TM,),
        in_specs=[tok_spec, pl.BlockSpec((TM, PLE_DIM), lambda t: (t, 0))]
                 + [_const_spec(c.shape) for c in consts],
        out_specs=tok_spec,
        out_shape=jax.ShapeDtypeStruct((n_tok, D_MODEL), F32),
        scratch_shapes=[pltpu.VMEM((TM, D_FF), BF16)],
        compiler_params=_params(),
        name="ffn2",
    )(h.reshape(n_tok, D_MODEL), p[i].reshape(n_tok, PLE_DIM), *consts)
    return h.reshape(bsz, seqlen, D_MODEL)
```

```python
import functools
import math

import jax
import jax.numpy as jnp
import numpy as np
from jax import lax
from jax.experimental import pallas as pl
from jax.experimental.pallas import tpu as pltpu

F32 = jnp.float32
BF16 = jnp.bfloat16

D_MODEL = 1024
D_FF = 2816
PLE_DIM = 256
SSM_WIDTH = 512
POOL_WIDTH = 512
SSM_GROUP = 16
SSM_GROUPS = 32
SSM_STATE = 64
POOL_WINDOWS = (2, 4, 8, 16)
FFN_RES = 0.5
EPS = 1e-6
GELU_C0 = math.sqrt(2.0 / math.pi)
GELU_C1 = 0.044715 * GELU_C0

LANES = 128
SUBLANES = 8
BF16_SUBLANES = 16
FOLD = LANES // SSM_GROUP
PAIRS = SSM_GROUPS // 2
PAIR_W = 2 * LANES
STATE_W = PAIRS * PAIR_W
SSM_VREGS = SSM_WIDTH // LANES

TM = 1024
TT = 128
KROWS = TT // FOLD
PITCH = TT + SUBLANES
FF_CHUNK = 256
POOL_TAIL = 32
VMEM_LIMIT = 56 * 1024 * 1024


def _rms(x, g):
    return x * lax.rsqrt(jnp.mean(x * x, axis=-1, keepdims=True) + EPS) * g


def _dot(a, b):
    return jnp.dot(a, b, preferred_element_type=F32)


def _rstd(x):
    return lax.rsqrt(jnp.mean(x * x, axis=-1, keepdims=True) + EPS)


def _swiglu(x, w1_ref, w3_ref, w2_ref, act_ref):
    xb = x.astype(BF16)
    r = jnp.broadcast_to(_rstd(x), (x.shape[0], FF_CHUNK))
    for c in range(D_FF // FF_CHUNK):
        sl = slice(c * FF_CHUNK, (c + 1) * FF_CHUNK)
        g = _dot(xb, w1_ref[:, sl]) * r
        u = _dot(xb, w3_ref[:, sl]) * r
        act_ref[:, sl] = (g * jax.nn.sigmoid(g) * u).astype(BF16)
    return _dot(act_ref[...], w2_ref[...])


def _ffn1_kernel(*refs, side_gain):
    x_ref, w1_ref, w3_ref, w2_ref = refs[:4]
    n_side = len(side_gain)
    n_side_in = n_side + sum(side_gain)
    side_in = iter(refs[4:4 + n_side_in])
    o_ref = refs[4 + n_side_in]
    side_out = refs[5 + n_side_in:5 + n_side_in + n_side]
    act_ref = refs[-1]

    x = x_ref[...]
    o_ref[...] = x + FFN_RES * _swiglu(x, w1_ref, w3_ref, w2_ref, act_ref)

    for has_gain, out_ref in zip(side_gain, side_out):
        w = next(side_in)[...]
        if has_gain:
            w = w * next(side_in)[...]
        out_ref[...] = w.astype(BF16)


def _ffn2_kernel(h_ref, p_ref, w1_ref, w3_ref, w2_ref, wg_ref, wp_ref, pnorm_ref, fnorm_ref,
                 o_ref, act_ref):
    h = h_ref[...]
    h = h + FFN_RES * _swiglu(h, w1_ref, w3_ref, w2_ref, act_ref)
    for hh in range(2):
        rs = slice(hh * (TM // 2), (hh + 1) * (TM // 2))
        hr = h[rs, :]
        gate = jax.nn.sigmoid(_dot(hr.astype(BF16), wg_ref[...]) * _rstd(hr))
        emb = _rms(_dot(p_ref[rs, :].astype(BF16), wp_ref[...]), pnorm_ref[...])
        o_ref[rs, :] = _rms(hr + gate * emb, fnorm_ref[...])


def _block_transpose(xs, lane_blk):
    xs = list(xs)
    for s in (4, 2, 1):
        upper = (lane_blk & s) != 0
        nxt = list(xs)
        for r in range(len(xs)):
            if r & s:
                continue
            a, b = xs[r], xs[r + s]
            nxt[r] = jnp.where(upper, pltpu.roll(b, SSM_GROUP * s, 1), a)
            nxt[r + s] = jnp.where(upper, b, pltpu.roll(a, LANES - SSM_GROUP * s, 1))
        xs = nxt
    return xs


def _mixer_kernel(h_ref, win_ref, btsup_ref, csup_ref, a8_ref,
                  dskip_ref, wglu_ref, bglu_ref, wpool_ref, wout_ref, o_ref,
                  state_ref, zs_ref, ys_ref, vbuf_ref, s2_ref, s4_ref, s8_ref):
    l = pl.program_id(0)
    nb = h_ref.shape[0]
    nr = TT + POOL_TAIL
    rows = nb * KROWS

    @pl.when(l == 0)
    def _():
        state_ref[...] = jnp.zeros_like(state_ref)
        vbuf_ref[:, 0:POOL_TAIL, :] = jnp.zeros((nb, POOL_TAIL, POOL_WIDTH), F32)

    @pl.when(l > 0)
    def _():
        vbuf_ref[:, 0:POOL_TAIL, :] = vbuf_ref[:, TT:nr, :]

    h = h_ref[...].reshape(nb * TT, D_MODEL)
    z = _dot(h.astype(BF16), win_ref[...]) * _rstd(h)

    for b in range(nb):
        for v in range(SSM_VREGS):
            zs_ref[v, b * PITCH:b * PITCH + TT, :] = z[b * TT:(b + 1) * TT, v * LANES:(v + 1) * LANES]
        vbuf_ref[b, POOL_TAIL:nr, :] = z[b * TT:(b + 1) * TT, SSM_WIDTH:]

    lane_blk = lax.broadcasted_iota(jnp.int32, (rows, LANES), 1) // SSM_GROUP
    for v in range(SSM_VREGS):
        toks = [jnp.concatenate([zs_ref[v, pl.ds(FOLD * k + j, nb, stride=PITCH), :]
                                 for k in range(KROWS)], axis=0) for j in range(FOLD)]
        folded = _block_transpose(toks, lane_blk)
        ysup = []
        for pp in range(FOLD // 2):
            pair = v * (FOLD // 2) + pp
            base = pair * PAIR_W
            usup = jnp.concatenate([folded[2 * pp], folded[2 * pp + 1]], axis=1).astype(BF16)
            xy = _dot(usup, btsup_ref[pair])
            ar = jnp.broadcast_to(a8_ref[0:1, base:base + LANES], (nb, LANES))
            ai = jnp.broadcast_to(a8_ref[0:1, base + LANES:base + PAIR_W], (nb, LANES))
            sr = state_ref[:, base:base + LANES]
            si = state_ref[:, base + LANES:base + PAIR_W]
            prev_r, prev_i = [], []
            for k in range(KROWS):
                prev_r.append(sr)
                prev_i.append(si)
                xr = xy[k * nb:(k + 1) * nb, 0:LANES]
                xi = xy[k * nb:(k + 1) * nb, LANES:PAIR_W]
                sr, si = ar * sr - ai * si + xr, ar * si + ai * sr + xi
            state_ref[:, base:base + LANES] = sr
            state_ref[:, base + LANES:base + PAIR_W] = si
            sprev = jnp.concatenate([jnp.concatenate(prev_r, axis=0),
                                     jnp.concatenate(prev_i, axis=0)], axis=1).astype(BF16)
            y = xy[:, PAIR_W:] + _dot(sprev, csup_ref[pair])
            ysup.append(y[:, :LANES])
            ysup.append(y[:, LANES:])
        outs = _block_transpose(ysup, lane_blk)
        for j in range(FOLD):
            for k in range(KROWS):
                ys_ref[v, pl.ds(FOLD * k + j, nb, stride=PITCH), :] = outs[j][k * nb:(k + 1) * nb, :]

    y = jnp.concatenate(
        [jnp.concatenate([ys_ref[v, b * PITCH:b * PITCH + TT, :] for v in range(SSM_VREGS)], axis=1)
         for b in range(nb)], axis=0)
    y = y + dskip_ref[...] * z[:, :SSM_WIDTH]
    y = y * (0.5 + 0.5 * jnp.tanh(y * (GELU_C0 + GELU_C1 * (y * y))))
    y = y * jax.nn.sigmoid(_dot(y.astype(BF16), wglu_ref[...]) + bglu_ref[...])

    s2_ref[:, 8:nr, :] = vbuf_ref[:, 8:nr, :] + vbuf_ref[:, 7:nr - 1, :]
    s4_ref[:, 16:nr, :] = s2_ref[:, 16:nr, LANES:] + s2_ref[:, 14:nr - 2, LANES:]
    s8_ref[:, 24:nr, :] = s4_ref[:, 24:nr, LANES:] + s4_ref[:, 20:nr - 4, LANES:]
    s16 = s8_ref[:, 32:nr, LANES:] + s8_ref[:, 24:nr - 8, LANES:]
    wins = [s2_ref[:, POOL_TAIL:nr, 0:LANES], s4_ref[:, POOL_TAIL:nr, 0:LANES],
            s8_ref[:, POOL_TAIL:nr, 0:LANES], s16]
    t = l * TT + lax.broadcasted_iota(jnp.int32, (1, TT, 1), 1)
    pooled = []
    for gi, w in enumerate(POOL_WINDOWS):
        inv_count = 1.0 / jnp.minimum(t + 1, w).astype(F32)
        vg = vbuf_ref[:, POOL_TAIL:nr, gi * LANES:(gi + 1) * LANES]
        pg = (wins[gi] * inv_count - vg).reshape(nb * TT, LANES).astype(BF16)
        pooled.append(_dot(pg, wpool_ref[gi]))
    y_pool = jnp.concatenate(pooled, axis=1)

    out = (h + _dot(y.astype(BF16), wout_ref[0:SSM_WIDTH, :]) * _rstd(y)
           + _dot(y_pool.astype(BF16), wout_ref[SSM_WIDTH:, :]) * _rstd(y_pool))
    o_ref[...] = out.reshape(nb, TT, D_MODEL)


def _pair_lanes(re, im):
    lead = re.shape[:-2]
    re = re.reshape(lead + (PAIRS, 2 * SSM_STATE))
    im = im.reshape(lead + (PAIRS, 2 * SSM_STATE))
    return jnp.concatenate([re, im], axis=-1).reshape(lead + (STATE_W,))


def _ssm_tables(a_re, a_im, log_dt, b_re, b_im, c_re, c_im):
    hi = lax.Precision.HIGHEST
    dt = jnp.exp(log_dt)[:, None]

    def apow(m):
        m = jnp.asarray(m, F32)[:, None, None]
        mag = jnp.exp(a_re * dt * m)
        return mag * jnp.cos(a_im * dt * m), mag * jnp.sin(a_im * dt * m)

    ar, ai = apow([1.0])
    ar, ai = ar[0], ai[0]
    den = a_re * a_re + a_im * a_im
    nr = ar - 1.0
    f_r = (nr * a_re + ai * a_im) / den
    f_i = (ai * a_re - nr * a_im) / den
    bb_r = f_r[..., None] * b_re - f_i[..., None] * b_im
    bb_i = f_r[..., None] * b_im + f_i[..., None] * b_re

    qr, qi = apow(np.arange(FOLD - 1, -1, -1))
    qr = jnp.transpose(qr, (1, 0, 2))[:, :, None, :]
    qi = jnp.transpose(qi, (1, 0, 2))[:, :, None, :]
    bt_r = jnp.transpose(bb_r, (0, 2, 1))[:, None]
    bt_i = jnp.transpose(bb_i, (0, 2, 1))[:, None]
    bs_r = (qr * bt_r - qi * bt_i).reshape(PAIRS, 2, LANES, SSM_STATE)
    bs_i = (qr * bt_i + qi * bt_r).reshape(PAIRS, 2, LANES, SSM_STATE)
    zb = jnp.zeros((PAIRS, LANES, SSM_STATE), F32)
    bsup = jnp.concatenate([
        jnp.concatenate([bs_r[:, 0], zb, bs_i[:, 0], zb], axis=-1),
        jnp.concatenate([zb, bs_r[:, 1], zb, bs_i[:, 1]], axis=-1)], axis=1)

    er, ei = apow(np.arange(1, FOLD + 1))
    er = jnp.transpose(er, (1, 0, 2))[:, :, None, :]
    ei = jnp.transpose(ei, (1, 0, 2))[:, :, None, :]
    m_r = (c_re[:, None] * er - c_im[:, None] * ei).reshape(SSM_GROUPS, LANES, SSM_STATE)
    m_i = (c_re[:, None] * ei + c_im[:, None] * er).reshape(SSM_GROUPS, LANES, SSM_STATE)
    ct_r = jnp.transpose(m_r, (0, 2, 1)).reshape(PAIRS, 2, SSM_STATE, LANES)
    ct_i = -jnp.transpose(m_i, (0, 2, 1)).reshape(PAIRS, 2, SSM_STATE, LANES)
    zc = jnp.zeros((PAIRS, SSM_STATE, LANES), F32)
    csup = jnp.concatenate([
        jnp.concatenate([ct_r[:, 0], zc], axis=-1),
        jnp.concatenate([zc, ct_r[:, 1]], axis=-1),
        jnp.concatenate([ct_i[:, 0], zc], axis=-1),
        jnp.concatenate([zc, ct_i[:, 1]], axis=-1)], axis=1)

    kr, ki = apow(np.arange(FOLD))
    kr = jnp.transpose(kr, (1, 0, 2))[:, :, None, :]
    ki = jnp.transpose(ki, (1, 0, 2))[:, :, None, :]
    w_r = c_re[:, None] * kr - c_im[:, None] * ki
    w_i = c_re[:, None] * ki + c_im[:, None] * kr
    kern = (jnp.einsum('gtpn,gnq->gtpq', w_r, bb_r, precision=hi)
            - jnp.einsum('gtpn,gnq->gtpq', w_i, bb_i, precision=hi))
    lag = np.arange(FOLD)[None, :] - np.arange(FOLD)[:, None]
    toe = jnp.take(kern, jnp.asarray(np.maximum(lag, 0).reshape(-1)), axis=1)
    toe = toe.reshape(SSM_GROUPS, FOLD, FOLD, SSM_GROUP, SSM_GROUP)
    toe = jnp.where(jnp.asarray(lag >= 0)[None, :, :, None, None], toe, 0.0)
    toe = jnp.transpose(toe, (0, 1, 4, 2, 3)).reshape(PAIRS, 2, LANES, LANES)
    zt = jnp.zeros((PAIRS, LANES, LANES), F32)
    tsup = jnp.concatenate([
        jnp.concatenate([toe[:, 0], zt], axis=-1),
        jnp.concatenate([zt, toe[:, 1]], axis=-1)], axis=1)

    btsup = jnp.concatenate([bsup, tsup], axis=-1)
    a8 = _pair_lanes(*apow([float(FOLD)]))
    return btsup.astype(BF16), csup.astype(BF16), a8


def _const_spec(shape):
    nd = len(shape)
    return pl.BlockSpec(shape, lambda *_: (0,) * nd)


def _params():
    return pltpu.CompilerParams(dimension_semantics=("arbitrary",), vmem_limit_bytes=VMEM_LIMIT)


def kernel(x, p, ffn1_norm, ffn1_w1, ffn1_w3, ffn1_w2, mix_norm, w_in, a_re, a_im, log_dt, b_re, b_im,
           c_re, c_im, d_skip, w_glu, b_glu, w_pool, pool_scale, ssm_out_norm, pool_out_norm, w_out,
           ffn2_norm, ffn2_w1, ffn2_w3, ffn2_w2, ple_gate_norm, w_ple_gate, w_ple_proj, ple_norm,
           final_norm):
    bsz, seqlen, _ = x.shape
    depth = ffn1_w1.shape[0]
    n_tok = bsz * seqlen
    assert n_tok % TM == 0 and seqlen % TT == 0 and bsz == SUBLANES

    row = lambda a: a.reshape(1, -1).astype(F32)
    gained = lambda w, g: (g.astype(F32)[:, None] * w.astype(F32)).astype(BF16)
    tok_spec = pl.BlockSpec((TM, D_MODEL), lambda t: (t, 0))
    seq_spec = pl.BlockSpec((bsz, TT, D_MODEL), lambda l: (0, l, 0))

    assert depth == 1
    i = 0

    out_gain = jnp.concatenate([ssm_out_norm[i], pool_out_norm[i]])
    side = [(ffn2_w1[i], ffn2_norm[i]), (ffn2_w3[i], ffn2_norm[i]), (ffn2_w2[i], None),
            (w_ple_gate[i], ple_gate_norm[i]), (w_ple_proj[i], None),
            (w_in[i], mix_norm[i]), (w_out[i], out_gain), (w_glu[i], None)]
    n_steps = n_tok // TM
    side_args, side_in_specs, side_out_specs, side_out_shapes = [], [], [], []
    for w, g in side:
        rows, cols = w.shape
        units = rows // BF16_SUBLANES
        n_blk = max(d for d in range(1, n_steps + 1) if units % d == 0)
        blk = rows // n_blk
        imap = functools.partial(lambda t, last: (jnp.minimum(t, last), 0), last=n_blk - 1)
        side_args.append(w.astype(F32))
        side_in_specs.append(pl.BlockSpec((blk, cols), imap))
        if g is not None:
            side_args.append(g.astype(F32).reshape(rows, 1))
            side_in_specs.append(pl.BlockSpec((blk, 1), imap))
        side_out_specs.append(pl.BlockSpec((blk, cols), imap))
        side_out_shapes.append(jax.ShapeDtypeStruct((rows, cols), BF16))

    h, w1b, w3b, w2b, wgb, wpb, winb, woutb, wglub = pl.pallas_call(
        functools.partial(_ffn1_kernel, side_gain=tuple(g is not None for _, g in side)),
        grid=(n_steps,),
        in_specs=[tok_spec, _const_spec((D_MODEL, D_FF)), _const_spec((D_MODEL, D_FF)),
                  _const_spec((D_FF, D_MODEL))] + side_in_specs,
        out_specs=[tok_spec] + side_out_specs,
        out_shape=[jax.ShapeDtypeStruct((n_tok, D_MODEL), F32)] + side_out_shapes,
        scratch_shapes=[pltpu.VMEM((TM, D_FF), BF16)],
        compiler_params=_params(),
        name="ffn1",
    )(x.reshape(n_tok, D_MODEL), gained(ffn1_w1[i], ffn1_norm[i]), gained(ffn1_w3[i], ffn1_norm[i]),
      ffn1_w2[i].astype(BF16), *side_args)

    btsup, csup, a8 = _ssm_tables(a_re[i], a_im[i], log_dt[i], b_re[i], b_im[i], c_re[i], c_im[i])
    pool_cols = pool_scale[i].astype(F32).reshape(len(POOL_WINDOWS), 1, LANES)
    consts = [winb, btsup, csup, a8, row(d_skip[i]), wglub, row(b_glu[i]),
              (w_pool[i].astype(F32) * pool_cols).astype(BF16), woutb]
    nr = TT + POOL_TAIL
    h = pl.pallas_call(
        _mixer_kernel,
        grid=(seqlen // TT,),
        in_specs=[seq_spec] + [_const_spec(c.shape) for c in consts],
        out_specs=seq_spec,
        out_shape=jax.ShapeDtypeStruct((bsz, seqlen, D_MODEL), F32),
        scratch_shapes=[pltpu.VMEM((bsz, STATE_W), F32),
                        pltpu.VMEM((SSM_VREGS, bsz * PITCH, LANES), F32),
                        pltpu.VMEM((SSM_VREGS, bsz * PITCH, LANES), F32),
                        pltpu.VMEM((bsz, nr, POOL_WIDTH), F32),
                        pltpu.VMEM((bsz, nr, POOL_WIDTH), F32),
                        pltpu.VMEM((bsz, nr, POOL_WIDTH - LANES), F32),
                        pltpu.VMEM((bsz, nr, POOL_WIDTH - 2 * LANES), F32)],
        compiler_params=_params(),
        name="mixer",
    )(h.reshape(bsz, seqlen, D_MODEL), *consts)

    consts = [w1b, w3b, w2b, wgb, wpb, row(ple_norm[i]), row(final_norm)]
    h = pl.pallas_call(
        _ffn2_kernel,
        grid=(n_tok // TM,),
        in_specs=[tok_spec, pl.BlockSpec((TM, PLE_DIM), lambda t: (t, 0))]
                 + [_const_spec(c.shape) for c in consts],
        out_specs=tok_spec,
        out_shape=jax.ShapeDtypeStruct((n_tok, D_MODEL), F32),
        scratch_shapes=[pltpu.VMEM((TM, D_FF), BF16)],
        compiler_params=_params(),
        name="ffn2",
    )(h.reshape(n_tok, D_MODEL), p[i].reshape(n_tok, PLE_DIM), *consts)
    return h.reshape(bsz, seqlen, D_MODEL)
```

```python
import functools
import math

import jax
import jax.numpy as jnp
import numpy as np
from jax import lax
from jax.experimental import pallas as pl
from jax.experimental.pallas import tpu as pltpu

F32 = jnp.float32
BF16 = jnp.bfloat16

D_MODEL = 1024
D_FF = 2816
PLE_DIM = 256
SSM_WIDTH = 512
POOL_WIDTH = 512
SSM_GROUP = 16
SSM_GROUPS = 32
SSM_STATE = 64
POOL_WINDOWS = (2, 4, 8, 16)
FFN_RES = 0.5
EPS = 1e-6
GELU_C0 = math.sqrt(2.0 / math.pi)
GELU_C1 = 0.044715 * GELU_C0

LANES = 128
SUBLANES = 8
BF16_SUBLANES = 16
FOLD = LANES // SSM_GROUP
PAIRS = SSM_GROUPS // 2
PAIR_W = 2 * LANES
STATE_W = PAIRS * PAIR_W
SSM_VREGS = SSM_WIDTH // LANES

TM = 1024
TT = 128
KROWS = TT // FOLD
PITCH = TT + SUBLANES
FF_CHUNK = 256
POOL_TAIL = 32
VMEM_LIMIT = 56 * 1024 * 1024


def _rms(x, g):
    return x * lax.rsqrt(jnp.mean(x * x, axis=-1, keepdims=True) + EPS) * g


def _dot(a, b):
    return jnp.dot(a, b, preferred_element_type=F32)


def _rstd(x):
    return lax.rsqrt(jnp.mean(x * x, axis=-1, keepdims=True) + EPS)


def _swiglu(x, w1_ref, w3_ref, w2_ref, act_ref):
    xb = x.astype(BF16)
    r = jnp.broadcast_to(_rstd(x), (x.shape[0], FF_CHUNK))
    for c in range(D_FF // FF_CHUNK):
        sl = slice(c * FF_CHUNK, (c + 1) * FF_CHUNK)
        g = _dot(xb, w1_ref[:, sl]) * r
        u = _dot(xb, w3_ref[:, sl]) * r
        act_ref[:, sl] = (g * jax.nn.sigmoid(g) * u).astype(BF16)
    return _dot(act_ref[...], w2_ref[...])


def _ffn1_kernel(*refs, side_gain):
    x_ref, w1_ref, w3_ref, w2_ref = refs[:4]
    n_side = len(side_gain)
    n_side_in = n_side + sum(side_gain)
    side_in = iter(refs[4:4 + n_side_in])
    o_ref = refs[4 + n_side_in]
    side_out = refs[5 + n_side_in:5 + n_side_in + n_side]
    act_ref = refs[-1]

    x = x_ref[...]
    o_ref[...] = x + FFN_RES * _swiglu(x, w1_ref, w3_ref, w2_ref, act_ref)

    for has_gain, out_ref in zip(side_gain, side_out):
        w = next(side_in)[...]
        if has_gain:
            w = w * next(side_in)[...]
        out_ref[...] = w.astype(BF16)


def _ffn2_kernel(h_ref, p_ref, w1_ref, w3_ref, w2_ref, wg_ref, wp_ref, pnorm_ref, fnorm_ref,
                 o_ref, act_ref):
    h = h_ref[...]
    h = h + FFN_RES * _swiglu(h, w1_ref, w3_ref, w2_ref, act_ref)
    for hh in range(2):
        rs = slice(hh * (TM // 2), (hh + 1) * (TM // 2))
        hr = h[rs, :]
        gate = jax.nn.sigmoid(_dot(hr.astype(BF16), wg_ref[...]) * _rstd(hr))
        emb = _rms(_dot(p_ref[rs, :].astype(BF16), wp_ref[...]), pnorm_ref[...])
        o_ref[rs, :] = _rms(hr + gate * emb, fnorm_ref[...])


def _block_transpose(xs, lane_blk):
    xs = list(xs)
    for s in (4, 2, 1):
        upper = (lane_blk & s) != 0
        nxt = list(xs)
        for r in range(len(xs)):
            if r & s:
                continue
            a, b = xs[r], xs[r + s]
            nxt[r] = jnp.where(upper, pltpu.roll(b, SSM_GROUP * s, 1), a)
            nxt[r + s] = jnp.where(upper, b, pltpu.roll(a, LANES - SSM_GROUP * s, 1))
        xs = nxt
    return xs


def _mixer_kernel(h_ref, win_ref, btsup_ref, csup_ref, a8_ref,
                  dskip_ref, wglu_ref, bglu_ref, wpool_ref, wout_ref, o_ref,
                  state_ref, zs_ref, ys_ref, vbuf_ref, s2_ref, s4_ref, s8_ref):
    l = pl.program_id(0)
    nb = h_ref.shape[0]
    nr = TT + POOL_TAIL
    rows = nb * KROWS

    @pl.when(l == 0)
    def _():
        state_ref[...] = jnp.zeros_like(state_ref)
        vbuf_ref[:, 0:POOL_TAIL, :] = jnp.zeros((nb, POOL_TAIL, POOL_WIDTH), F32)

    @pl.when(l > 0)
    def _():
        vbuf_ref[:, 0:POOL_TAIL, :] = vbuf_ref[:, TT:nr, :]

    h = h_ref[...].reshape(nb * TT, D_MODEL)
    z = _dot(h.astype(BF16), win_ref[...]) * _rstd(h)

    for b in range(nb):
        for v in range(SSM_VREGS):
            zs_ref[v, b * PITCH:b * PITCH + TT, :] = z[b * TT:(b + 1) * TT, v * LANES:(v + 1) * LANES]
        vbuf_ref[b, POOL_TAIL:nr, :] = z[b * TT:(b + 1) * TT, SSM_WIDTH:]

    lane_blk = lax.broadcasted_iota(jnp.int32, (rows, LANES), 1) // SSM_GROUP
    for v in range(SSM_VREGS):
        toks = [jnp.concatenate([zs_ref[v, pl.ds(FOLD * k + j, nb, stride=PITCH), :]
                                 for k in range(KROWS)], axis=0) for j in range(FOLD)]
        folded = _block_transpose(toks, lane_blk)
        ysup = []
        for pp in range(FOLD // 2):
            pair = v * (FOLD // 2) + pp
            base = pair * PAIR_W
            usup = jnp.concatenate([folded[2 * pp], folded[2 * pp + 1]], axis=1).astype(BF16)
            xy = _dot(usup, btsup_ref[pair])
            ar = jnp.broadcast_to(a8_ref[0:1, base:base + LANES], (nb, LANES))
            ai = jnp.broadcast_to(a8_ref[0:1, base + LANES:base + PAIR_W], (nb, LANES))
            sr = state_ref[:, base:base + LANES]
            si = state_ref[:, base + LANES:base + PAIR_W]
            prev_r, prev_i = [], []
            for k in range(KROWS):
                prev_r.append(sr)
                prev_i.append(si)
                xr = xy[k * nb:(k + 1) * nb, 0:LANES]
                xi = xy[k * nb:(k + 1) * nb, LANES:PAIR_W]
                sr, si = ar * sr - ai * si + xr, ar * si + ai * sr + xi
            state_ref[:, base:base + LANES] = sr
            state_ref[:, base + LANES:base + PAIR_W] = si
            sprev = jnp.concatenate([jnp.concatenate(prev_r, axis=0),
                                     jnp.concatenate(prev_i, axis=0)], axis=1).astype(BF16)
            y = xy[:, PAIR_W:] + _dot(sprev, csup_ref[pair])
            ysup.append(y[:, :LANES])
            ysup.append(y[:, LANES:])
        outs = _block_transpose(ysup, lane_blk)
        for j in range(FOLD):
            for k in range(KROWS):
                ys_ref[v, pl.ds(FOLD * k + j, nb, stride=PITCH), :] = outs[j][k * nb:(k + 1) * nb, :]

    y = jnp.concatenate(
        [jnp.concatenate([ys_ref[v, b * PITCH:b * PITCH + TT, :] for v in range(SSM_VREGS)], axis=1)
         for b in range(nb)], axis=0)
    y = y + dskip_ref[...] * z[:, :SSM_WIDTH]
    y = y * (0.5 + 0.5 * jnp.tanh(y * (GELU_C0 + GELU_C1 * (y * y))))
    y = y * jax.nn.sigmoid(_dot(y.astype(BF16), wglu_ref[...]) + bglu_ref[...])

    s2_ref[:, 8:nr, :] = vbuf_ref[:, 8:nr, :] + vbuf_ref[:, 7:nr - 1, :]
    s4_ref[:, 16:nr, :] = s2_ref[:, 16:nr, LANES:] + s2_ref[:, 14:nr - 2, LANES:]
    s8_ref[:, 24:nr, :] = s4_ref[:, 24:nr, LANES:] + s4_ref[:, 20:nr - 4, LANES:]
    s16 = s8_ref[:, 32:nr, LANES:] + s8_ref[:, 24:nr - 8, LANES:]
    wins = [s2_ref[:, POOL_TAIL:nr, 0:LANES], s4_ref[:, POOL_TAIL:nr, 0:LANES],
            s8_ref[:, POOL_TAIL:nr, 0:LANES], s16]
    t = l * TT + lax.broadcasted_iota(jnp.int32, (1, TT, 1), 1)
    pooled = []
    for gi, w in enumerate(POOL_WINDOWS):
        inv_count = 1.0 / jnp.minimum(t + 1, w).astype(F32)
        vg = vbuf_ref[:, POOL_TAIL:nr, gi * LANES:(gi + 1) * LANES]
        pg = (wins[gi] * inv_count - vg).reshape(nb * TT, LANES).astype(BF16)
        pooled.append(_dot(pg, wpool_ref[gi]))
    y_pool = jnp.concatenate(pooled, axis=1)

    out = (h + _dot(y.astype(BF16), wout_ref[0:SSM_WIDTH, :]) * _rstd(y)
           + _dot(y_pool.astype(BF16), wout_ref[SSM_WIDTH:, :]) * _rstd(y_pool))
    o_ref[...] = out.reshape(nb, TT, D_MODEL)


def _cmul(xr, xi, yr, yi):
    return xr * yr - xi * yi, xr * yi + xi * yr


def _dot_nt(a, b):
    return lax.dot_general(a, b, (((1,), (1,)), ((), ())), precision=lax.Precision.HIGHEST,
                           preferred_element_type=F32)


def _tables_kernel(lam_ref, bt_ref, ct_ref, btsup_ref, csup_ref, a8_ref, pwr_ref, pwi_ref):
    lam_r, lam_i, dt = lam_ref[0:1, :], lam_ref[1:2, :], jnp.exp(lam_ref[2:3, :])
    m = lax.broadcasted_iota(jnp.int32, (2 * FOLD, LANES), 0).astype(F32)
    mag = jnp.exp(m * (lam_r * dt))
    pwr_ref[...] = mag * jnp.cos(m * (lam_i * dt))
    pwi_ref[...] = mag * jnp.sin(m * (lam_i * dt))
    ar, ai = pwr_ref[1:2, :], pwi_ref[1:2, :]
    den = lam_r * lam_r + lam_i * lam_i
    nr = ar - 1.0
    f_r = (nr * lam_r + ai * lam_i) / den
    f_i = (ai * lam_r - nr * lam_i) / den
    bb_r, bb_i = _cmul(f_r, f_i, bt_ref[0], bt_ref[1])
    c_r, c_i = ct_ref[0], ct_ref[1]

    def tiled(x):
        return jnp.concatenate([x] * FOLD, axis=0)

    def power_rows(first, step):
        ks = [first + step * j for j in range(FOLD)]
        return (jnp.concatenate([jnp.broadcast_to(pwr_ref[k:k + 1, :], (SSM_GROUP, LANES)) for k in ks], axis=0),
                jnp.concatenate([jnp.broadcast_to(pwi_ref[k:k + 1, :], (SSM_GROUP, LANES)) for k in ks], axis=0))

    lane = lax.broadcasted_iota(jnp.int32, (LANES, LANES), 1)
    row = lax.broadcasted_iota(jnp.int32, (LANES, LANES), 0)
    g0_lanes = lane < SSM_STATE
    g0_rows = row < SSM_STATE
    zero = jnp.zeros((LANES, LANES), F32)

    s_r, s_i = _cmul(tiled(bb_r), tiled(bb_i), *power_rows(FOLD - 1, -1))
    btsup_ref[0:LANES, 0:LANES] = jnp.where(g0_lanes, s_r, zero).astype(BF16)
    btsup_ref[0:LANES, LANES:PAIR_W] = jnp.where(g0_lanes, s_i, zero).astype(BF16)
    btsup_ref[LANES:PAIR_W, 0:LANES] = jnp.where(g0_lanes, zero, s_r).astype(BF16)
    btsup_ref[LANES:PAIR_W, LANES:PAIR_W] = jnp.where(g0_lanes, zero, s_i).astype(BF16)

    m_r, m_i = _cmul(tiled(c_r), tiled(c_i), *power_rows(1, 1))
    mt_r, mt_i = m_r.T, -m_i.T
    csup_ref[0:LANES, 0:LANES] = jnp.where(g0_rows, mt_r, zero).astype(BF16)
    csup_ref[0:LANES, LANES:PAIR_W] = jnp.where(g0_rows, zero, mt_r).astype(BF16)
    csup_ref[LANES:PAIR_W, 0:LANES] = jnp.where(g0_rows, mt_i, zero).astype(BF16)
    csup_ref[LANES:PAIR_W, LANES:PAIR_W] = jnp.where(g0_rows, zero, mt_i).astype(BF16)

    w_r, w_i = _cmul(tiled(c_r), tiled(c_i), *power_rows(0, 1))
    lane16 = lax.broadcasted_iota(jnp.int32, (SSM_GROUP, LANES), 1)
    for gl in range(2):
        mine = (lane16 < SSM_STATE) if gl == 0 else (lane16 >= SSM_STATE)
        kern = (_dot_nt(jnp.where(mine, bb_r, 0.0), w_r) - _dot_nt(jnp.where(mine, bb_i, 0.0), w_i))
        toe = jnp.concatenate(
            [jnp.where(lane16 >= SSM_GROUP * i, pltpu.roll(kern, SSM_GROUP * i, 1), 0.0) if i else kern
             for i in range(FOLD)], axis=0)
        rows = slice(gl * LANES, (gl + 1) * LANES)
        btsup_ref[rows, PAIR_W + gl * LANES:PAIR_W + (gl + 1) * LANES] = toe.astype(BF16)
        btsup_ref[rows, PAIR_W + (1 - gl) * LANES:PAIR_W + (2 - gl) * LANES] = jnp.zeros((LANES, LANES), BF16)

    a8_ref[...] = jnp.concatenate([pwr_ref[FOLD:FOLD + 1, :], pwi_ref[FOLD:FOLD + 1, :]], axis=1)


def _ssm_tables(a_re, a_im, log_dt, b_re, b_im, c_re, c_im):
    lanes_gn = lambda a: a.astype(F32).reshape(PAIRS, 1, LANES)
    lam = jnp.concatenate(
        [lanes_gn(a_re), lanes_gn(a_im),
         lanes_gn(jnp.broadcast_to(log_dt[:, None], (SSM_GROUPS, SSM_STATE))),
         jnp.zeros((PAIRS, SUBLANES - 3, LANES), F32)], axis=1)
    bt = jnp.stack([b_re, b_im]).astype(F32).reshape(2, PAIRS, 2, SSM_STATE, SSM_GROUP)
    bt = jnp.transpose(bt, (1, 0, 4, 2, 3)).reshape(PAIRS, 2, SSM_GROUP, LANES)
    ct = jnp.stack([c_re, c_im]).astype(F32).reshape(2, PAIRS, 2, SSM_GROUP, SSM_STATE)
    ct = jnp.transpose(ct, (1, 0, 3, 2, 4)).reshape(PAIRS, 2, SSM_GROUP, LANES)
    return pl.pallas_call(
        _tables_kernel,
        grid=(PAIRS,),
        in_specs=[pl.BlockSpec((None, SUBLANES, LANES), lambda q: (q, 0, 0)),
                  pl.BlockSpec((None, 2, SSM_GROUP, LANES), lambda q: (q, 0, 0, 0)),
                  pl.BlockSpec((None, 2, SSM_GROUP, LANES), lambda q: (q, 0, 0, 0))],
        out_specs=[pl.BlockSpec((None, PAIR_W, 2 * PAIR_W), lambda q: (q, 0, 0)),
                   pl.BlockSpec((None, PAIR_W, PAIR_W), lambda q: (q, 0, 0)),
                   pl.BlockSpec((1, PAIR_W), lambda q: (0, q))],
        out_shape=[jax.ShapeDtypeStruct((PAIRS, PAIR_W, 2 * PAIR_W), BF16),
                   jax.ShapeDtypeStruct((PAIRS, PAIR_W, PAIR_W), BF16),
                   jax.ShapeDtypeStruct((1, STATE_W), F32)],
        scratch_shapes=[pltpu.VMEM((2 * FOLD, LANES), F32), pltpu.VMEM((2 * FOLD, LANES), F32)],
        compiler_params=pltpu.CompilerParams(dimension_semantics=("arbitrary",)),
        name="s5_tables",
    )(lam, bt, ct)


def _const_spec(shape):
    nd = len(shape)
    return pl.BlockSpec(shape, lambda *_: (0,) * nd)


def _params():
    return pltpu.CompilerParams(dimension_semantics=("arbitrary",), vmem_limit_bytes=VMEM_LIMIT)


def kernel(x, p, ffn1_norm, ffn1_w1, ffn1_w3, ffn1_w2, mix_norm, w_in, a_re, a_im, log_dt, b_re, b_im,
           c_re, c_im, d_skip, w_glu, b_glu, w_pool, pool_scale, ssm_out_norm, pool_out_norm, w_out,
           ffn2_norm, ffn2_w1, ffn2_w3, ffn2_w2, ple_gate_norm, w_ple_gate, w_ple_proj, ple_norm,
           final_norm):
    bsz, seqlen, _ = x.shape
    depth = ffn1_w1.shape[0]
    n_tok = bsz * seqlen
    assert n_tok % TM == 0 and seqlen % TT == 0 and bsz == SUBLANES

    row = lambda a: a.reshape(1, -1).astype(F32)
    gained = lambda w, g: (g.astype(F32)[:, None] * w.astype(F32)).astype(BF16)
    tok_spec = pl.BlockSpec((TM, D_MODEL), lambda t: (t, 0))
    seq_spec = pl.BlockSpec((bsz, TT, D_MODEL), lambda l: (0, l, 0))

    assert depth == 1
    i = 0

    out_gain = jnp.concatenate([ssm_out_norm[i], pool_out_norm[i]])
    side = [(ffn2_w1[i], ffn2_norm[i]), (ffn2_w3[i], ffn2_norm[i]), (ffn2_w2[i], None),
            (w_ple_gate[i], ple_gate_norm[i]), (w_ple_proj[i], None),
            (w_in[i], mix_norm[i]), (w_out[i], out_gain), (w_glu[i], None)]
    n_steps = n_tok // TM
    side_args, side_in_specs, side_out_specs, side_out_shapes = [], [], [], []
    for w, g in side:
        rows, cols = w.shape
        units = rows // BF16_SUBLANES
        n_blk = max(d for d in range(1, n_steps + 1) if units % d == 0)
        blk = rows // n_blk
        imap = functools.partial(lambda t, last: (jnp.minimum(t, last), 0), last=n_blk - 1)
        side_args.append(w.astype(F32))
        side_in_specs.append(pl.BlockSpec((blk, cols), imap))
        if g is not None:
            side_args.append(g.astype(F32).reshape(rows, 1))
            side_in_specs.append(pl.BlockSpec((blk, 1), imap))
        side_out_specs.append(pl.BlockSpec((blk, cols), imap))
        side_out_shapes.append(jax.ShapeDtypeStruct((rows, cols), BF16))

    h, w1b, w3b, w2b, wgb, wpb, winb, woutb, wglub = pl.pallas_call(
        functools.partial(_ffn1_kernel, side_gain=tuple(g is not None for _, g in side)),
        grid=(n_steps,),
        in_specs=[tok_spec, _const_spec((D_MODEL, D_FF)), _const_spec((D_MODEL, D_FF)),
                  _const_spec((D_FF, D_MODEL))] + side_in_specs,
        out_specs=[tok_spec] + side_out_specs,
        out_shape=[jax.ShapeDtypeStruct((n_tok, D_MODEL), F32)] + side_out_shapes,
        scratch_shapes=[pltpu.VMEM((TM, D_FF), BF16)],
        compiler_params=_params(),
        name="ffn1",
    )(x.reshape(n_tok, D_MODEL), gained(ffn1_w1[i], ffn1_norm[i]), gained(ffn1_w3[i], ffn1_norm[i]),
      ffn1_w2[i].astype(BF16), *side_args)

    btsup, csup, a8 = _ssm_tables(a_re[i], a_im[i], log_dt[i], b_re[i], b_im[i], c_re[i], c_im[i])
    pool_cols = pool_scale[i].astype(F32).reshape(len(POOL_WINDOWS), 1, LANES)
    consts = [winb, btsup, csup, a8, row(d_skip[i]), wglub, row(b_glu[i]),
              (w_pool[i].astype(F32) * pool_cols).astype(BF16), woutb]
    nr = TT + POOL_TAIL
    h = pl.pallas_call(
        _mixer_kernel,
        grid=(seqlen // TT,),
        in_specs=[seq_spec] + [_const_spec(c.shape) for c in consts],
        out_specs=seq_spec,
        out_shape=jax.ShapeDtypeStruct((bsz, seqlen, D_MODEL), F32),
        scratch_shapes=[pltpu.VMEM((bsz, STATE_W), F32),
                        pltpu.VMEM((SSM_VREGS, bsz * PITCH, LANES), F32),
                        pltpu.VMEM((SSM_VREGS, bsz * PITCH, LANES), F32),
                        pltpu.VMEM((bsz, nr, POOL_WIDTH), F32),
                        pltpu.VMEM((bsz, nr, POOL_WIDTH), F32),
                        pltpu.VMEM((bsz, nr, POOL_WIDTH - LANES), F32),
                        pltpu.VMEM((bsz, nr, POOL_WIDTH - 2 * LANES), F32)],
        compiler_params=_params(),
        name="mixer",
    )(h.reshape(bsz, seqlen, D_MODEL), *consts)

    consts = [w1b, w3b, w2b, wgb, wpb, row(ple_norm[i]), row(final_norm)]
    h = pl.pallas_call(
        _ffn2_kernel,
        grid=(n_tok // TM,),
        in_specs=[tok_spec, pl.BlockSpec((TM, PLE_DIM), lambda t: (t, 0))]
                 + [_const_spec(c.shape) for c in consts],
        out_specs=tok_spec,
        out_shape=jax.ShapeDtypeStruct((n_tok, D_MODEL), F32),
        scratch_shapes=[pltpu.VMEM((TM, D_FF), BF16)],
        compiler_params=_params(),
        name="ffn2",
    )(h.reshape(n_tok, D_MODEL), p[i].reshape(n_tok, PLE_DIM), *consts)
    return h.reshape(bsz, seqlen, D_MODEL)
```

```python
import functools
import math

import jax
import jax.numpy as jnp
import numpy as np
from jax import lax
from jax.experimental import pallas as pl
from jax.experimental.pallas import tpu as pltpu

F32 = jnp.float32
BF16 = jnp.bfloat16

D_MODEL = 1024
D_FF = 2816
PLE_DIM = 256
SSM_WIDTH = 512
POOL_WIDTH = 512
SSM_GROUP = 16
SSM_GROUPS = 32
SSM_STATE = 64
POOL_WINDOWS = (2, 4, 8, 16)
FFN_RES = 0.5
EPS = 1e-6
GELU_C0 = math.sqrt(2.0 / math.pi)
GELU_C1 = 0.044715 * GELU_C0

LANES = 128
SUBLANES = 8
BF16_SUBLANES = 16
FOLD = LANES // SSM_GROUP
PAIRS = SSM_GROUPS // 2
PAIR_W = 2 * LANES
STATE_W = PAIRS * PAIR_W
SSM_VREGS = SSM_WIDTH // LANES

TM = 1024
TT = 128
KROWS = TT // FOLD
PITCH = TT + SUBLANES
FF_CHUNK = 256
TAIL_PARTS = 4
POOL_TAIL = 32
VMEM_LIMIT = 56 * 1024 * 1024


def _rms(x, g):
    return x * lax.rsqrt(jnp.mean(x * x, axis=-1, keepdims=True) + EPS) * g


def _dot(a, b):
    return jnp.dot(a, b, preferred_element_type=F32)


def _rstd(x):
    return lax.rsqrt(jnp.mean(x * x, axis=-1, keepdims=True) + EPS)


def _swiglu(x, w1_ref, w3_ref, w2_ref, act_ref):
    xb = x.astype(BF16)
    r = jnp.broadcast_to(_rstd(x), (x.shape[0], FF_CHUNK))
    for c in range(D_FF // FF_CHUNK):
        sl = slice(c * FF_CHUNK, (c + 1) * FF_CHUNK)
        g = _dot(xb, w1_ref[:, sl]) * r
        u = _dot(xb, w3_ref[:, sl]) * r
        act_ref[:, sl] = (g * jax.nn.sigmoid(g) * u).astype(BF16)
    return _dot(act_ref[...], w2_ref[...])


def _ffn1_kernel(*refs, side_gain):
    x_ref, w1_ref, w3_ref, w2_ref = refs[:4]
    n_side = len(side_gain)
    n_side_in = n_side + sum(side_gain)
    side_in = iter(refs[4:4 + n_side_in])
    o_ref = refs[4 + n_side_in]
    side_out = refs[5 + n_side_in:5 + n_side_in + n_side]
    act_ref = refs[-1]

    x = x_ref[...]
    o_ref[...] = x + FFN_RES * _swiglu(x, w1_ref, w3_ref, w2_ref, act_ref)

    for has_gain, out_ref in zip(side_gain, side_out):
        w = next(side_in)[...]
        if has_gain:
            w = w * next(side_in)[...]
        out_ref[...] = w.astype(BF16)


def _ffn2_kernel(h_ref, p_ref, w1_ref, w3_ref, w2_ref, wg_ref, wp_ref, pnorm_ref, fnorm_ref,
                 o_ref, act_ref):
    emb = _rms(_dot(p_ref[...].astype(BF16), wp_ref[...]), pnorm_ref[...])
    h = h_ref[...]
    h = h + FFN_RES * _swiglu(h, w1_ref, w3_ref, w2_ref, act_ref)
    for hh in range(TAIL_PARTS):
        rs = slice(hh * (TM // TAIL_PARTS), (hh + 1) * (TM // TAIL_PARTS))
        hr = h[rs, :]
        gate = jax.nn.sigmoid(_dot(hr.astype(BF16), wg_ref[...]) * _rstd(hr))
        o_ref[rs, :] = _rms(hr + gate * emb[rs, :], fnorm_ref[...])


def _block_transpose(xs, lane_blk):
    xs = list(xs)
    for s in (4, 2, 1):
        upper = (lane_blk & s) != 0
        nxt = list(xs)
        for r in range(len(xs)):
            if r & s:
                continue
            a, b = xs[r], xs[r + s]
            nxt[r] = jnp.where(upper, pltpu.roll(b, SSM_GROUP * s, 1), a)
            nxt[r + s] = jnp.where(upper, b, pltpu.roll(a, LANES - SSM_GROUP * s, 1))
        xs = nxt
    return xs


def _mixer_kernel(h_ref, win_ref, btsup_ref, csup_ref, a8_ref,
                  dskip_ref, wglu_ref, bglu_ref, wpool_ref, wout_ref, o_ref,
                  state_ref, zs_ref, ys_ref, vbuf_ref, s2_ref, s4_ref, s8_ref):
    l = pl.program_id(0)
    nb = h_ref.shape[0]
    nr = TT + POOL_TAIL
    rows = nb * KROWS

    @pl.when(l == 0)
    def _():
        state_ref[...] = jnp.zeros_like(state_ref)
        vbuf_ref[:, 0:POOL_TAIL, :] = jnp.zeros((nb, POOL_TAIL, POOL_WIDTH), F32)

    @pl.when(l > 0)
    def _():
        vbuf_ref[:, 0:POOL_TAIL, :] = vbuf_ref[:, TT:nr, :]

    h = h_ref[...].reshape(nb * TT, D_MODEL)
    z = _dot(h.astype(BF16), win_ref[...]) * _rstd(h)

    for b in range(nb):
        for v in range(SSM_VREGS):
            zs_ref[v, b * PITCH:b * PITCH + TT, :] = z[b * TT:(b + 1) * TT, v * LANES:(v + 1) * LANES]
        vbuf_ref[b, POOL_TAIL:nr, :] = z[b * TT:(b + 1) * TT, SSM_WIDTH:]

    lane_blk = lax.broadcasted_iota(jnp.int32, (rows, LANES), 1) // SSM_GROUP
    for v in range(SSM_VREGS):
        toks = [jnp.concatenate([zs_ref[v, pl.ds(FOLD * k + j, nb, stride=PITCH), :]
                                 for k in range(KROWS)], axis=0) for j in range(FOLD)]
        folded = _block_transpose(toks, lane_blk)
        ysup = []
        for pp in range(FOLD // 2):
            pair = v * (FOLD // 2) + pp
            base = pair * PAIR_W
            usup = jnp.concatenate([folded[2 * pp], folded[2 * pp + 1]], axis=1).astype(BF16)
            xy = _dot(usup, btsup_ref[pair])
            ar = jnp.broadcast_to(a8_ref[0:1, base:base + LANES], (nb, LANES))
            ai = jnp.broadcast_to(a8_ref[0:1, base + LANES:base + PAIR_W], (nb, LANES))
            sr = state_ref[:, base:base + LANES]
            si = state_ref[:, base + LANES:base + PAIR_W]
            prev_r, prev_i = [], []
            for k in range(KROWS):
                prev_r.append(sr)
                prev_i.append(si)
                xr = xy[k * nb:(k + 1) * nb, 0:LANES]
                xi = xy[k * nb:(k + 1) * nb, LANES:PAIR_W]
                sr, si = ar * sr - ai * si + xr, ar * si + ai * sr + xi
            state_ref[:, base:base + LANES] = sr
            state_ref[:, base + LANES:base + PAIR_W] = si
            sprev = jnp.concatenate([jnp.concatenate(prev_r, axis=0),
                                     jnp.concatenate(prev_i, axis=0)], axis=1).astype(BF16)
            y = xy[:, PAIR_W:] + _dot(sprev, csup_ref[pair])
            ysup.append(y[:, :LANES])
            ysup.append(y[:, LANES:])
        outs = _block_transpose(ysup, lane_blk)
        for j in range(FOLD):
            for k in range(KROWS):
                ys_ref[v, pl.ds(FOLD * k + j, nb, stride=PITCH), :] = outs[j][k * nb:(k + 1) * nb, :]

    y = jnp.concatenate(
        [jnp.concatenate([ys_ref[v, b * PITCH:b * PITCH + TT, :] for v in range(SSM_VREGS)], axis=1)
         for b in range(nb)], axis=0)
    y = y + dskip_ref[...] * z[:, :SSM_WIDTH]
    y = y * (0.5 + 0.5 * jnp.tanh(y * (GELU_C0 + GELU_C1 * (y * y))))
    y = y * jax.nn.sigmoid(_dot(y.astype(BF16), wglu_ref[...]) + bglu_ref[...])

    s2_ref[:, 8:nr, :] = vbuf_ref[:, 8:nr, :] + vbuf_ref[:, 7:nr - 1, :]
    s4_ref[:, 16:nr, :] = s2_ref[:, 16:nr, LANES:] + s2_ref[:, 14:nr - 2, LANES:]
    s8_ref[:, 24:nr, :] = s4_ref[:, 24:nr, LANES:] + s4_ref[:, 20:nr - 4, LANES:]
    s16 = s8_ref[:, 32:nr, LANES:] + s8_ref[:, 24:nr - 8, LANES:]
    wins = [s2_ref[:, POOL_TAIL:nr, 0:LANES], s4_ref[:, POOL_TAIL:nr, 0:LANES],
            s8_ref[:, POOL_TAIL:nr, 0:LANES], s16]
    t = l * TT + lax.broadcasted_iota(jnp.int32, (1, TT, 1), 1)
    pooled = []
    for gi, w in enumerate(POOL_WINDOWS):
        inv_count = 1.0 / jnp.minimum(t + 1, w).astype(F32)
        vg = vbuf_ref[:, POOL_TAIL:nr, gi * LANES:(gi + 1) * LANES]
        pg = (wins[gi] * inv_count - vg).reshape(nb * TT, LANES).astype(BF16)
        pooled.append(_dot(pg, wpool_ref[gi]))
    y_pool = jnp.concatenate(pooled, axis=1)

    out = (h + _dot(y.astype(BF16), wout_ref[0:SSM_WIDTH, :]) * _rstd(y)
           + _dot(y_pool.astype(BF16), wout_ref[SSM_WIDTH:, :]) * _rstd(y_pool))
    o_ref[...] = out.reshape(nb, TT, D_MODEL)


def _cmul(xr, xi, yr, yi):
    return xr * yr - xi * yi, xr * yi + xi * yr


def _dot_nt(a, b):
    return lax.dot_general(a, b, (((1,), (1,)), ((), ())), precision=lax.Precision.HIGHEST,
                           preferred_element_type=F32)


def _tables_kernel(lam_ref, bt_ref, ct_ref, btsup_ref, csup_ref, a8_ref, pwr_ref, pwi_ref):
    lam_r, lam_i, dt = lam_ref[0:1, :], lam_ref[1:2, :], jnp.exp(lam_ref[2:3, :])
    m = lax.broadcasted_iota(jnp.int32, (2 * FOLD, LANES), 0).astype(F32)
    mag = jnp.exp(m * (lam_r * dt))
    pwr_ref[...] = mag * jnp.cos(m * (lam_i * dt))
    pwi_ref[...] = mag * jnp.sin(m * (lam_i * dt))
    ar, ai = pwr_ref[1:2, :], pwi_ref[1:2, :]
    den = lam_r * lam_r + lam_i * lam_i
    nr = ar - 1.0
    f_r = (nr * lam_r + ai * lam_i) / den
    f_i = (ai * lam_r - nr * lam_i) / den
    bb_r, bb_i = _cmul(f_r, f_i, bt_ref[0], bt_ref[1])
    c_r, c_i = ct_ref[0], ct_ref[1]

    def tiled(x):
        return jnp.concatenate([x] * FOLD, axis=0)

    def power_rows(first, step):
        ks = [first + step * j for j in range(FOLD)]
        return (jnp.concatenate([jnp.broadcast_to(pwr_ref[k:k + 1, :], (SSM_GROUP, LANES)) for k in ks], axis=0),
                jnp.concatenate([jnp.broadcast_to(pwi_ref[k:k + 1, :], (SSM_GROUP, LANES)) for k in ks], axis=0))

    lane = lax.broadcasted_iota(jnp.int32, (LANES, LANES), 1)
    row = lax.broadcasted_iota(jnp.int32, (LANES, LANES), 0)
    g0_lanes = lane < SSM_STATE
    g0_rows = row < SSM_STATE
    zero = jnp.zeros((LANES, LANES), F32)

    s_r, s_i = _cmul(tiled(bb_r), tiled(bb_i), *power_rows(FOLD - 1, -1))
    btsup_ref[0:LANES, 0:LANES] = jnp.where(g0_lanes, s_r, zero).astype(BF16)
    btsup_ref[0:LANES, LANES:PAIR_W] = jnp.where(g0_lanes, s_i, zero).astype(BF16)
    btsup_ref[LANES:PAIR_W, 0:LANES] = jnp.where(g0_lanes, zero, s_r).astype(BF16)
    btsup_ref[LANES:PAIR_W, LANES:PAIR_W] = jnp.where(g0_lanes, zero, s_i).astype(BF16)

    m_r, m_i = _cmul(tiled(c_r), tiled(c_i), *power_rows(1, 1))
    mt_r, mt_i = m_r.T, -m_i.T
    csup_ref[0:LANES, 0:LANES] = jnp.where(g0_rows, mt_r, zero).astype(BF16)
    csup_ref[0:LANES, LANES:PAIR_W] = jnp.where(g0_rows, zero, mt_r).astype(BF16)
    csup_ref[LANES:PAIR_W, 0:LANES] = jnp.where(g0_rows, mt_i, zero).astype(BF16)
    csup_ref[LANES:PAIR_W, LANES:PAIR_W] = jnp.where(g0_rows, zero, mt_i).astype(BF16)

    w_r, w_i = _cmul(tiled(c_r), tiled(c_i), *power_rows(0, 1))
    lane16 = lax.broadcasted_iota(jnp.int32, (SSM_GROUP, LANES), 1)
    for gl in range(2):
        mine = (lane16 < SSM_STATE) if gl == 0 else (lane16 >= SSM_STATE)
        kern = (_dot_nt(jnp.where(mine, bb_r, 0.0), w_r) - _dot_nt(jnp.where(mine, bb_i, 0.0), w_i))
        toe = jnp.concatenate(
            [jnp.where(lane16 >= SSM_GROUP * i, pltpu.roll(kern, SSM_GROUP * i, 1), 0.0) if i else kern
             for i in range(FOLD)], axis=0)
        rows = slice(gl * LANES, (gl + 1) * LANES)
        btsup_ref[rows, PAIR_W + gl * LANES:PAIR_W + (gl + 1) * LANES] = toe.astype(BF16)
        btsup_ref[rows, PAIR_W + (1 - gl) * LANES:PAIR_W + (2 - gl) * LANES] = jnp.zeros((LANES, LANES), BF16)

    a8_ref[...] = jnp.concatenate([pwr_ref[FOLD:FOLD + 1, :], pwi_ref[FOLD:FOLD + 1, :]], axis=1)


def _ssm_tables(a_re, a_im, log_dt, b_re, b_im, c_re, c_im):
    lanes_gn = lambda a: a.astype(F32).reshape(PAIRS, 1, LANES)
    lam = jnp.concatenate(
        [lanes_gn(a_re), lanes_gn(a_im),
         lanes_gn(jnp.broadcast_to(log_dt[:, None], (SSM_GROUPS, SSM_STATE))),
         jnp.zeros((PAIRS, SUBLANES - 3, LANES), F32)], axis=1)
    bt = jnp.stack([b_re, b_im]).astype(F32).reshape(2, PAIRS, 2, SSM_STATE, SSM_GROUP)
    bt = jnp.transpose(bt, (1, 0, 4, 2, 3)).reshape(PAIRS, 2, SSM_GROUP, LANES)
    ct = jnp.stack([c_re, c_im]).astype(F32).reshape(2, PAIRS, 2, SSM_GROUP, SSM_STATE)
    ct = jnp.transpose(ct, (1, 0, 3, 2, 4)).reshape(PAIRS, 2, SSM_GROUP, LANES)
    return pl.pallas_call(
        _tables_kernel,
        grid=(PAIRS,),
        in_specs=[pl.BlockSpec((None, SUBLANES, LANES), lambda q: (q, 0, 0)),
                  pl.BlockSpec((None, 2, SSM_GROUP, LANES), lambda q: (q, 0, 0, 0)),
                  pl.BlockSpec((None, 2, SSM_GROUP, LANES), lambda q: (q, 0, 0, 0))],
        out_specs=[pl.BlockSpec((None, PAIR_W, 2 * PAIR_W), lambda q: (q, 0, 0)),
                   pl.BlockSpec((None, PAIR_W, PAIR_W), lambda q: (q, 0, 0)),
                   pl.BlockSpec((1, PAIR_W), lambda q: (0, q))],
        out_shape=[jax.ShapeDtypeStruct((PAIRS, PAIR_W, 2 * PAIR_W), BF16),
                   jax.ShapeDtypeStruct((PAIRS, PAIR_W, PAIR_W), BF16),
                   jax.ShapeDtypeStruct((1, STATE_W), F32)],
        scratch_shapes=[pltpu.VMEM((2 * FOLD, LANES), F32), pltpu.VMEM((2 * FOLD, LANES), F32)],
        compiler_params=pltpu.CompilerParams(dimension_semantics=("arbitrary",)),
        name="s5_tables",
    )(lam, bt, ct)


def _const_spec(shape):
    nd = len(shape)
    return pl.BlockSpec(shape, lambda *_: (0,) * nd)


def _params():
    return pltpu.CompilerParams(dimension_semantics=("arbitrary",), vmem_limit_bytes=VMEM_LIMIT)


def kernel(x, p, ffn1_norm, ffn1_w1, ffn1_w3, ffn1_w2, mix_norm, w_in, a_re, a_im, log_dt, b_re, b_im,
           c_re, c_im, d_skip, w_glu, b_glu, w_pool, pool_scale, ssm_out_norm, pool_out_norm, w_out,
           ffn2_norm, ffn2_w1, ffn2_w3, ffn2_w2, ple_gate_norm, w_ple_gate, w_ple_proj, ple_norm,
           final_norm):
    bsz, seqlen, _ = x.shape
    depth = ffn1_w1.shape[0]
    n_tok = bsz * seqlen
    assert n_tok % TM == 0 and seqlen % TT == 0 and bsz == SUBLANES

    row = lambda a: a.reshape(1, -1).astype(F32)
    gained = lambda w, g: (g.astype(F32)[:, None] * w.astype(F32)).astype(BF16)
    tok_spec = pl.BlockSpec((TM, D_MODEL), lambda t: (t, 0))
    seq_spec = pl.BlockSpec((bsz, TT, D_MODEL), lambda l: (0, l, 0))

    assert depth == 1
    i = 0

    out_gain = jnp.concatenate([ssm_out_norm[i], pool_out_norm[i]])
    side = [(ffn2_w1[i], ffn2_norm[i]), (ffn2_w3[i], ffn2_norm[i]), (ffn2_w2[i], None),
            (w_ple_gate[i], ple_gate_norm[i]), (w_ple_proj[i], None),
            (w_in[i], mix_norm[i]), (w_out[i], out_gain), (w_glu[i], None)]
    n_steps = n_tok // TM
    side_args, side_in_specs, side_out_specs, side_out_shapes = [], [], [], []
    for w, g in side:
        rows, cols = w.shape
        units = rows // BF16_SUBLANES
        n_blk = max(d for d in range(1, n_steps + 1) if units % d == 0)
        blk = rows // n_blk
        imap = functools.partial(lambda t, last: (jnp.minimum(t, last), 0), last=n_blk - 1)
        side_args.append(w.astype(F32))
        side_in_specs.append(pl.BlockSpec((blk, cols), imap))
        if g is not None:
            side_args.append(g.astype(F32).reshape(rows, 1))
            side_in_specs.append(pl.BlockSpec((blk, 1), imap))
        side_out_specs.append(pl.BlockSpec((blk, cols), imap))
        side_out_shapes.append(jax.ShapeDtypeStruct((rows, cols), BF16))

    h, w1b, w3b, w2b, wgb, wpb, winb, woutb, wglub = pl.pallas_call(
        functools.partial(_ffn1_kernel, side_gain=tuple(g is not None for _, g in side)),
        grid=(n_steps,),
        in_specs=[tok_spec, _const_spec((D_MODEL, D_FF)), _const_spec((D_MODEL, D_FF)),
                  _const_spec((D_FF, D_MODEL))] + side_in_specs,
        out_specs=[tok_spec] + side_out_specs,
        out_shape=[jax.ShapeDtypeStruct((n_tok, D_MODEL), F32)] + side_out_shapes,
        scratch_shapes=[pltpu.VMEM((TM, D_FF), BF16)],
        compiler_params=_params(),
        name="ffn1",
    )(x.reshape(n_tok, D_MODEL), gained(ffn1_w1[i], ffn1_norm[i]), gained(ffn1_w3[i], ffn1_norm[i]),
      ffn1_w2[i].astype(BF16), *side_args)

    btsup, csup, a8 = _ssm_tables(a_re[i], a_im[i], log_dt[i], b_re[i], b_im[i], c_re[i], c_im[i])
    pool_cols = pool_scale[i].astype(F32).reshape(len(POOL_WINDOWS), 1, LANES)
    consts = [winb, btsup, csup, a8, row(d_skip[i]), wglub, row(b_glu[i]),
              (w_pool[i].astype(F32) * pool_cols).astype(BF16), woutb]
    nr = TT + POOL_TAIL
    h = pl.pallas_call(
        _mixer_kernel,
        grid=(seqlen // TT,),
        in_specs=[seq_spec] + [_const_spec(c.shape) for c in consts],
        out_specs=seq_spec,
        out_shape=jax.ShapeDtypeStruct((bsz, seqlen, D_MODEL), F32),
        scratch_shapes=[pltpu.VMEM((bsz, STATE_W), F32),
                        pltpu.VMEM((SSM_VREGS, bsz * PITCH, LANES), F32),
                        pltpu.VMEM((SSM_VREGS, bsz * PITCH, LANES), F32),
                        pltpu.VMEM((bsz, nr, POOL_WIDTH), F32),
                        pltpu.VMEM((bsz, nr, POOL_WIDTH), F32),
                        pltpu.VMEM((bsz, nr, POOL_WIDTH - LANES), F32),
                        pltpu.VMEM((bsz, nr, POOL_WIDTH - 2 * LANES), F32)],
        compiler_params=_params(),
        name="mixer",
    )(h.reshape(bsz, seqlen, D_MODEL), *consts)

    consts = [w1b, w3b, w2b, wgb, wpb, row(ple_norm[i]), row(final_norm)]
    h = pl.pallas_call(
        _ffn2_kernel,
        grid=(n_tok // TM,),
        in_specs=[tok_spec, pl.BlockSpec((TM, PLE_DIM), lambda t: (t, 0))]
                 + [_const_spec(c.shape) for c in consts],
        out_specs=tok_spec,
        out_shape=jax.ShapeDtypeStruct((n_tok, D_MODEL), F32),
        scratch_shapes=[pltpu.VMEM((TM, D_FF), BF16)],
        compiler_params=_params(),
        name="ffn2",
    )(h.reshape(n_tok, D_MODEL), p[i].reshape(n_tok, PLE_DIM), *consts)
    return h.reshape(bsz, seqlen, D_MODEL)
```

```python
import functools
import math

import jax
import jax.numpy as jnp
from jax import lax
from jax.experimental import pallas as pl
from jax.experimental.pallas import tpu as pltpu

F32 = jnp.float32
BF16 = jnp.bfloat16

D_MODEL = 1024
D_FF = 2816
PLE_DIM = 256
SSM_WIDTH = 512
POOL_WIDTH = 512
SSM_GROUP = 16
SSM_GROUPS = 32
SSM_STATE = 64
POOL_WINDOWS = (2, 4, 8, 16)
FFN_RES = 0.5
EPS = 1e-6
GELU_C0 = math.sqrt(2.0 / math.pi)
GELU_C1 = 0.044715 * GELU_C0

LANES = 128
SUBLANES = 8
BF16_SUBLANES = 16
V7X_VMEM_BYTES = 64 * 1024 * 1024
FOLD = LANES // SSM_GROUP
PAIRS = SSM_GROUPS // 2
PAIR_W = 2 * LANES
STATE_W = PAIRS * PAIR_W
SSM_VREGS = SSM_WIDTH // LANES

TM = 1024
TT = 128
KROWS = TT // FOLD
PITCH = TT + SUBLANES
FF_CHUNK = 256
FFN1_PREP = 8
TAIL_PARTS = 4
POOL_TAIL = 32
VMEM_LIMIT = V7X_VMEM_BYTES * 7 // 8


def _rms(x, g):
    return x * lax.rsqrt(jnp.mean(x * x, axis=-1, keepdims=True) + EPS) * g


def _dot(a, b):
    return jnp.dot(a, b, preferred_element_type=F32)


def _rstd(x):
    return lax.rsqrt(jnp.mean(x * x, axis=-1, keepdims=True) + EPS)


def _swiglu(x, w1_ref, w3_ref, w2_ref, act_ref):
    xb = x.astype(BF16)
    r = jnp.broadcast_to(_rstd(x), (x.shape[0], FF_CHUNK))
    for c in range(D_FF // FF_CHUNK):
        sl = slice(c * FF_CHUNK, (c + 1) * FF_CHUNK)
        g = _dot(xb, w1_ref[:, sl]) * r
        u = _dot(xb, w3_ref[:, sl]) * r
        act_ref[:, sl] = (g * jax.nn.sigmoid(g) * u).astype(BF16)
    return _dot(act_ref[...], w2_ref[...])


def _ffn1_kernel(*refs, side_gain):
    x_ref, g_ref, w1_ref, w3_ref, w2_ref = refs[:5]
    n_side = len(side_gain)
    n_side_in = n_side + sum(side_gain)
    side_in = iter(refs[5:5 + n_side_in])
    o_ref = refs[5 + n_side_in]
    side_out = refs[6 + n_side_in:6 + n_side_in + n_side]
    w1s_ref, w3s_ref, w2s_ref, act_ref = refs[-4:]
    t = pl.program_id(0)

    @pl.when(t < FFN1_PREP)
    def _():
        r13 = pl.multiple_of(t * w1_ref.shape[0], w1_ref.shape[0])
        w1s_ref[pl.ds(r13, w1_ref.shape[0]), :] = (w1_ref[...] * g_ref[...]).astype(BF16)
        w3s_ref[pl.ds(r13, w3_ref.shape[0]), :] = (w3_ref[...] * g_ref[...]).astype(BF16)
        r2 = pl.multiple_of(t * w2_ref.shape[0], w2_ref.shape[0])
        w2s_ref[pl.ds(r2, w2_ref.shape[0]), :] = w2_ref[...].astype(BF16)

    @pl.when(t >= FFN1_PREP)
    def _():
        x = x_ref[...]
        o_ref[...] = x + FFN_RES * _swiglu(x, w1s_ref, w3s_ref, w2s_ref, act_ref)

    for has_gain, out_ref in zip(side_gain, side_out):
        w = next(side_in)[...]
        if has_gain:
            w = w * next(side_in)[...]
        out_ref[...] = w.astype(BF16)


def _ffn2_kernel(h_ref, p_ref, w1_ref, w3_ref, w2_ref, wg_ref, wp_ref, pnorm_ref, fnorm_ref,
                 o_ref, act_ref):
    emb = _rms(_dot(p_ref[...].astype(BF16), wp_ref[...]), pnorm_ref[...])
    h = h_ref[...]
    h = h + FFN_RES * _swiglu(h, w1_ref, w3_ref, w2_ref, act_ref)
    for hh in range(TAIL_PARTS):
        rs = slice(hh * (TM // TAIL_PARTS), (hh + 1) * (TM // TAIL_PARTS))
        hr = h[rs, :]
        gate = jax.nn.sigmoid(_dot(hr.astype(BF16), wg_ref[...]) * _rstd(hr))
        o_ref[rs, :] = _rms(hr + gate * emb[rs, :], fnorm_ref[...])


def _block_transpose(xs, lane_blk):
    xs = list(xs)
    for s in (4, 2, 1):
        upper = (lane_blk & s) != 0
        nxt = list(xs)
        for r in range(len(xs)):
            if r & s:
                continue
            a, b = xs[r], xs[r + s]
            nxt[r] = jnp.where(upper, pltpu.roll(b, SSM_GROUP * s, 1), a)
            nxt[r + s] = jnp.where(upper, b, pltpu.roll(a, LANES - SSM_GROUP * s, 1))
        xs = nxt
    return xs


def _mixer_kernel(h_ref, win_ref, btsup_ref, csup_ref, a8_ref,
                  dskip_ref, wglu_ref, bglu_ref, wpool_ref, wout_ref, o_ref,
                  state_ref, zs_ref, ys_ref, vbuf_ref, s2_ref, s4_ref, s8_ref):
    l = pl.program_id(0)
    nb = h_ref.shape[0]
    nr = TT + POOL_TAIL
    rows = nb * KROWS

    @pl.when(l == 0)
    def _():
        state_ref[...] = jnp.zeros_like(state_ref)
        vbuf_ref[:, 0:POOL_TAIL, :] = jnp.zeros((nb, POOL_TAIL, POOL_WIDTH), F32)

    @pl.when(l > 0)
    def _():
        vbuf_ref[:, 0:POOL_TAIL, :] = vbuf_ref[:, TT:nr, :]

    h = h_ref[...].reshape(nb * TT, D_MODEL)
    z = _dot(h.astype(BF16), win_ref[...]) * _rstd(h)

    for b in range(nb):
        for v in range(SSM_VREGS):
            zs_ref[v, b * PITCH:b * PITCH + TT, :] = z[b * TT:(b + 1) * TT, v * LANES:(v + 1) * LANES]
        vbuf_ref[b, POOL_TAIL:nr, :] = z[b * TT:(b + 1) * TT, SSM_WIDTH:]

    lane_blk = lax.broadcasted_iota(jnp.int32, (rows, LANES), 1) // SSM_GROUP
    for v in range(SSM_VREGS):
        toks = [jnp.concatenate([zs_ref[v, pl.ds(FOLD * k + j, nb, stride=PITCH), :]
                                 for k in range(KROWS)], axis=0) for j in range(FOLD)]
        folded = _block_transpose(toks, lane_blk)
        ysup = []
        for pp in range(FOLD // 2):
            pair = v * (FOLD // 2) + pp
            base = pair * PAIR_W
            usup = jnp.concatenate([folded[2 * pp], folded[2 * pp + 1]], axis=1).astype(BF16)
            xy = _dot(usup, btsup_ref[pair])
            ar = jnp.broadcast_to(a8_ref[0:1, base:base + LANES], (nb, LANES))
            ai = jnp.broadcast_to(a8_ref[0:1, base + LANES:base + PAIR_W], (nb, LANES))
            sr = state_ref[:, base:base + LANES]
            si = state_ref[:, base + LANES:base + PAIR_W]
            prev_r, prev_i = [], []
            for k in range(KROWS):
                prev_r.append(sr)
                prev_i.append(si)
                xr = xy[k * nb:(k + 1) * nb, 0:LANES]
                xi = xy[k * nb:(k + 1) * nb, LANES:PAIR_W]
                sr, si = ar * sr - ai * si + xr, ar * si + ai * sr + xi
            state_ref[:, base:base + LANES] = sr
            state_ref[:, base + LANES:base + PAIR_W] = si
            sprev = jnp.concatenate([jnp.concatenate(prev_r, axis=0),
                                     jnp.concatenate(prev_i, axis=0)], axis=1).astype(BF16)
            y = xy[:, PAIR_W:] + _dot(sprev, csup_ref[pair])
            ysup.append(y[:, :LANES])
            ysup.append(y[:, LANES:])
        outs = _block_transpose(ysup, lane_blk)
        for j in range(FOLD):
            for k in range(KROWS):
                ys_ref[v, pl.ds(FOLD * k + j, nb, stride=PITCH), :] = outs[j][k * nb:(k + 1) * nb, :]

    y = jnp.concatenate(
        [jnp.concatenate([ys_ref[v, b * PITCH:b * PITCH + TT, :] for v in range(SSM_VREGS)], axis=1)
         for b in range(nb)], axis=0)
    y = y + dskip_ref[...] * z[:, :SSM_WIDTH]
    y = y * (0.5 + 0.5 * jnp.tanh(y * (GELU_C0 + GELU_C1 * (y * y))))
    y = y * jax.nn.sigmoid(_dot(y.astype(BF16), wglu_ref[...]) + bglu_ref[...])

    s2_ref[:, 8:nr, :] = vbuf_ref[:, 8:nr, :] + vbuf_ref[:, 7:nr - 1, :]
    s4_ref[:, 16:nr, :] = s2_ref[:, 16:nr, LANES:] + s2_ref[:, 14:nr - 2, LANES:]
    s8_ref[:, 24:nr, :] = s4_ref[:, 24:nr, LANES:] + s4_ref[:, 20:nr - 4, LANES:]
    s16 = s8_ref[:, 32:nr, LANES:] + s8_ref[:, 24:nr - 8, LANES:]
    wins = [s2_ref[:, POOL_TAIL:nr, 0:LANES], s4_ref[:, POOL_TAIL:nr, 0:LANES],
            s8_ref[:, POOL_TAIL:nr, 0:LANES], s16]
    t = l * TT + lax.broadcasted_iota(jnp.int32, (1, TT, 1), 1)
    pooled = []
    for gi, w in enumerate(POOL_WINDOWS):
        inv_count = 1.0 / jnp.minimum(t + 1, w).astype(F32)
        vg = vbuf_ref[:, POOL_TAIL:nr, gi * LANES:(gi + 1) * LANES]
        pg = (wins[gi] * inv_count - vg).reshape(nb * TT, LANES).astype(BF16)
        pooled.append(_dot(pg, wpool_ref[gi]))
    y_pool = jnp.concatenate(pooled, axis=1)

    yn = jnp.concatenate([(y * _rstd(y)).astype(BF16), (y_pool * _rstd(y_pool)).astype(BF16)], axis=1)
    o_ref[...] = (h + _dot(yn, wout_ref[...])).reshape(nb, TT, D_MODEL)


def _cmul(xr, xi, yr, yi):
    return xr * yr - xi * yi, xr * yi + xi * yr


def _dot_nt(a, b):
    return lax.dot_general(a, b, (((1,), (1,)), ((), ())), precision=lax.Precision.HIGHEST,
                           preferred_element_type=F32)


def _tables_kernel(lam_ref, bt_ref, ct_ref, btsup_ref, csup_ref, a8_ref, pwr_ref, pwi_ref):
    lam_r, lam_i, dt = lam_ref[0:1, :], lam_ref[1:2, :], jnp.exp(lam_ref[2:3, :])
    m = lax.broadcasted_iota(jnp.int32, (2 * FOLD, LANES), 0).astype(F32)
    mag = jnp.exp(m * (lam_r * dt))
    pwr_ref[...] = mag * jnp.cos(m * (lam_i * dt))
    pwi_ref[...] = mag * jnp.sin(m * (lam_i * dt))
    ar, ai = pwr_ref[1:2, :], pwi_ref[1:2, :]
    den = lam_r * lam_r + lam_i * lam_i
    nr = ar - 1.0
    f_r = (nr * lam_r + ai * lam_i) / den
    f_i = (ai * lam_r - nr * lam_i) / den
    bb_r, bb_i = _cmul(f_r, f_i, bt_ref[0], bt_ref[1])
    c_r, c_i = ct_ref[0], ct_ref[1]

    def tiled(x):
        return jnp.concatenate([x] * FOLD, axis=0)

    def power_rows(first, step):
        ks = [first + step * j for j in range(FOLD)]
        return (jnp.concatenate([jnp.broadcast_to(pwr_ref[k:k + 1, :], (SSM_GROUP, LANES)) for k in ks], axis=0),
                jnp.concatenate([jnp.broadcast_to(pwi_ref[k:k + 1, :], (SSM_GROUP, LANES)) for k in ks], axis=0))

    lane = lax.broadcasted_iota(jnp.int32, (LANES, LANES), 1)
    row = lax.broadcasted_iota(jnp.int32, (LANES, LANES), 0)
    g0_lanes = lane < SSM_STATE
    g0_rows = row < SSM_STATE
    zero = jnp.zeros((LANES, LANES), F32)

    s_r, s_i = _cmul(tiled(bb_r), tiled(bb_i), *power_rows(FOLD - 1, -1))
    btsup_ref[0:LANES, 0:LANES] = jnp.where(g0_lanes, s_r, zero).astype(BF16)
    btsup_ref[0:LANES, LANES:PAIR_W] = jnp.where(g0_lanes, s_i, zero).astype(BF16)
    btsup_ref[LANES:PAIR_W, 0:LANES] = jnp.where(g0_lanes, zero, s_r).astype(BF16)
    btsup_ref[LANES:PAIR_W, LANES:PAIR_W] = jnp.where(g0_lanes, zero, s_i).astype(BF16)

    m_r, m_i = _cmul(tiled(c_r), tiled(c_i), *power_rows(1, 1))
    mt_r, mt_i = m_r.T, -m_i.T
    csup_ref[0:LANES, 0:LANES] = jnp.where(g0_rows, mt_r, zero).astype(BF16)
    csup_ref[0:LANES, LANES:PAIR_W] = jnp.where(g0_rows, zero, mt_r).astype(BF16)
    csup_ref[LANES:PAIR_W, 0:LANES] = jnp.where(g0_rows, mt_i, zero).astype(BF16)
    csup_ref[LANES:PAIR_W, LANES:PAIR_W] = jnp.where(g0_rows, zero, mt_i).astype(BF16)

    w_r, w_i = _cmul(tiled(c_r), tiled(c_i), *power_rows(0, 1))
    lane16 = lax.broadcasted_iota(jnp.int32, (SSM_GROUP, LANES), 1)
    for gl in range(2):
        mine = (lane16 < SSM_STATE) if gl == 0 else (lane16 >= SSM_STATE)
        kern = (_dot_nt(jnp.where(mine, bb_r, 0.0), w_r) - _dot_nt(jnp.where(mine, bb_i, 0.0), w_i))
        toe = jnp.concatenate(
            [jnp.where(lane16 >= SSM_GROUP * i, pltpu.roll(kern, SSM_GROUP * i, 1), 0.0) if i else kern
             for i in range(FOLD)], axis=0)
        rows = slice(gl * LANES, (gl + 1) * LANES)
        btsup_ref[rows, PAIR_W + gl * LANES:PAIR_W + (gl + 1) * LANES] = toe.astype(BF16)
        btsup_ref[rows, PAIR_W + (1 - gl) * LANES:PAIR_W + (2 - gl) * LANES] = jnp.zeros((LANES, LANES), BF16)

    a8_ref[...] = jnp.concatenate([pwr_ref[FOLD:FOLD + 1, :], pwi_ref[FOLD:FOLD + 1, :]], axis=1)


def _ssm_tables(a_re, a_im, log_dt, b_re, b_im, c_re, c_im):
    lanes_gn = lambda a: a.astype(F32).reshape(PAIRS, 1, LANES)
    lam = jnp.concatenate(
        [lanes_gn(a_re), lanes_gn(a_im),
         lanes_gn(jnp.broadcast_to(log_dt[:, None], (SSM_GROUPS, SSM_STATE))),
         jnp.zeros((PAIRS, SUBLANES - 3, LANES), F32)], axis=1)
    bt = jnp.stack([b_re, b_im]).astype(F32).reshape(2, PAIRS, 2, SSM_STATE, SSM_GROUP)
    bt = jnp.transpose(bt, (1, 0, 4, 2, 3)).reshape(PAIRS, 2, SSM_GROUP, LANES)
    ct = jnp.stack([c_re, c_im]).astype(F32).reshape(2, PAIRS, 2, SSM_GROUP, SSM_STATE)
    ct = jnp.transpose(ct, (1, 0, 3, 2, 4)).reshape(PAIRS, 2, SSM_GROUP, LANES)
    return pl.pallas_call(
        _tables_kernel,
        grid=(PAIRS,),
        in_specs=[pl.BlockSpec((None, SUBLANES, LANES), lambda q: (q, 0, 0)),
                  pl.BlockSpec((None, 2, SSM_GROUP, LANES), lambda q: (q, 0, 0, 0)),
                  pl.BlockSpec((None, 2, SSM_GROUP, LANES), lambda q: (q, 0, 0, 0))],
        out_specs=[pl.BlockSpec((None, PAIR_W, 2 * PAIR_W), lambda q: (q, 0, 0)),
                   pl.BlockSpec((None, PAIR_W, PAIR_W), lambda q: (q, 0, 0)),
                   pl.BlockSpec((1, PAIR_W), lambda q: (0, q))],
        out_shape=[jax.ShapeDtypeStruct((PAIRS, PAIR_W, 2 * PAIR_W), BF16),
                   jax.ShapeDtypeStruct((PAIRS, PAIR_W, PAIR_W), BF16),
                   jax.ShapeDtypeStruct((1, STATE_W), F32)],
        scratch_shapes=[pltpu.VMEM((2 * FOLD, LANES), F32), pltpu.VMEM((2 * FOLD, LANES), F32)],
        compiler_params=pltpu.CompilerParams(dimension_semantics=("arbitrary",)),
        name="s5_tables",
    )(lam, bt, ct)


def _const_spec(shape):
    nd = len(shape)
    return pl.BlockSpec(shape, lambda *_: (0,) * nd)


def _params():
    return pltpu.CompilerParams(dimension_semantics=("arbitrary",), vmem_limit_bytes=VMEM_LIMIT)


def kernel(x, p, ffn1_norm, ffn1_w1, ffn1_w3, ffn1_w2, mix_norm, w_in, a_re, a_im, log_dt, b_re, b_im,
           c_re, c_im, d_skip, w_glu, b_glu, w_pool, pool_scale, ssm_out_norm, pool_out_norm, w_out,
           ffn2_norm, ffn2_w1, ffn2_w3, ffn2_w2, ple_gate_norm, w_ple_gate, w_ple_proj, ple_norm,
           final_norm):
    bsz, seqlen, _ = x.shape
    depth = ffn1_w1.shape[0]
    n_tok = bsz * seqlen
    assert n_tok % TM == 0 and seqlen % TT == 0 and bsz == SUBLANES

    row = lambda a: a.reshape(1, -1).astype(F32)
    tok_spec = pl.BlockSpec((TM, D_MODEL), lambda t: (t, 0))
    seq_spec = pl.BlockSpec((bsz, TT, D_MODEL), lambda l: (0, l, 0))

    assert depth == 1
    i = 0

    out_gain = jnp.concatenate([ssm_out_norm[i], pool_out_norm[i]])
    side = [(ffn2_w1[i], ffn2_norm[i]), (ffn2_w3[i], ffn2_norm[i]), (ffn2_w2[i], None),
            (w_ple_gate[i], ple_gate_norm[i]), (w_ple_proj[i], None),
            (w_in[i], mix_norm[i]), (w_out[i], out_gain), (w_glu[i], None)]
    n_steps = n_tok // TM + FFN1_PREP
    side_args, side_in_specs, side_out_specs, side_out_shapes = [], [], [], []
    for w, g in side:
        rows, cols = w.shape
        units = rows // BF16_SUBLANES
        n_blk = max(d for d in range(1, n_steps + 1) if units % d == 0)
        blk = rows // n_blk
        imap = functools.partial(lambda t, last: (jnp.minimum(t, last), 0), last=n_blk - 1)
        side_args.append(w.astype(F32))
        side_in_specs.append(pl.BlockSpec((blk, cols), imap))
        if g is not None:
            side_args.append(g.astype(F32).reshape(rows, 1))
            side_in_specs.append(pl.BlockSpec((blk, 1), imap))
        side_out_specs.append(pl.BlockSpec((blk, cols), imap))
        side_out_shapes.append(jax.ShapeDtypeStruct((rows, cols), BF16))

    prep_map = lambda t: (jnp.minimum(t, FFN1_PREP - 1), 0)
    ffn1_tok_spec = pl.BlockSpec((TM, D_MODEL), lambda t: (jnp.maximum(t - FFN1_PREP, 0), 0))
    h, w1b, w3b, w2b, wgb, wpb, winb, woutb, wglub = pl.pallas_call(
        functools.partial(_ffn1_kernel, side_gain=tuple(g is not None for _, g in side)),
        grid=(n_steps,),
        in_specs=[ffn1_tok_spec, pl.BlockSpec((D_MODEL // FFN1_PREP, 1), prep_map),
                  pl.BlockSpec((D_MODEL // FFN1_PREP, D_FF), prep_map),
                  pl.BlockSpec((D_MODEL // FFN1_PREP, D_FF), prep_map),
                  pl.BlockSpec((D_FF // FFN1_PREP, D_MODEL), prep_map)] + side_in_specs,
        out_specs=[ffn1_tok_spec] + side_out_specs,
        out_shape=[jax.ShapeDtypeStruct((n_tok, D_MODEL), F32)] + side_out_shapes,
        scratch_shapes=[pltpu.VMEM((D_MODEL, D_FF), BF16), pltpu.VMEM((D_MODEL, D_FF), BF16),
                        pltpu.VMEM((D_FF, D_MODEL), BF16), pltpu.VMEM((TM, D_FF), BF16)],
        compiler_params=_params(),
        name="ffn1",
    )(x.reshape(n_tok, D_MODEL), ffn1_norm[i].astype(F32).reshape(D_MODEL, 1), ffn1_w1[i].astype(F32),
      ffn1_w3[i].astype(F32), ffn1_w2[i].astype(F32), *side_args)

    btsup, csup, a8 = _ssm_tables(a_re[i], a_im[i], log_dt[i], b_re[i], b_im[i], c_re[i], c_im[i])
    pool_cols = pool_scale[i].astype(F32).reshape(len(POOL_WINDOWS), 1, LANES)
    consts = [winb, btsup, csup, a8, row(d_skip[i]), wglub, row(b_glu[i]),
              (w_pool[i].astype(F32) * pool_cols).astype(BF16), woutb]
    nr = TT + POOL_TAIL
    h = pl.pallas_call(
        _mixer_kernel,
        grid=(seqlen // TT,),
        in_specs=[seq_spec] + [_const_spec(c.shape) for c in consts],
        out_specs=seq_spec,
        out_shape=jax.ShapeDtypeStruct((bsz, seqlen, D_MODEL), F32),
        scratch_shapes=[pltpu.VMEM((bsz, STATE_W), F32),
                        pltpu.VMEM((SSM_VREGS, bsz * PITCH, LANES), F32),
                        pltpu.VMEM((SSM_VREGS, bsz * PITCH, LANES), F32),
                        pltpu.VMEM((bsz, nr, POOL_WIDTH), F32),
                        pltpu.VMEM((bsz, nr, POOL_WIDTH), F32),
                        pltpu.VMEM((bsz, nr, POOL_WIDTH - LANES), F32),
                        pltpu.VMEM((bsz, nr, POOL_WIDTH - 2 * LANES), F32)],
        compiler_params=_params(),
        name="mixer",
    )(h.reshape(bsz, seqlen, D_MODEL), *consts)

    consts = [w1b, w3b, w2b, wgb, wpb, row(ple_norm[i]), row(final_norm)]
    h = pl.pallas_call(
        _ffn2_kernel,
        grid=(n_tok // TM,),
        in_specs=[tok_spec, pl.BlockSpec((TM, PLE_DIM), lambda t: (t, 0))]
                 + [_const_spec(c.shape) for c in consts],
        out_specs=tok_spec,
        out_shape=jax.ShapeDtypeStruct((n_tok, D_MODEL), F32),
        scratch_shapes=[pltpu.VMEM((TM, D_FF), BF16)],
        compiler_params=_params(),
        name="ffn2",
    )(h.reshape(n_tok, D_MODEL), p[i].reshape(n_tok, PLE_DIM), *consts)
    return h.reshape(bsz, seqlen, D_MODEL)
```

```python
import functools
import math

import jax
import jax.numpy as jnp
from jax import lax
from jax.experimental import pallas as pl
from jax.experimental.pallas import tpu as pltpu

F32 = jnp.float32
BF16 = jnp.bfloat16

D_MODEL = 1024
D_FF = 2816
PLE_DIM = 256
SSM_WIDTH = 512
POOL_WIDTH = 512
SSM_GROUP = 16
SSM_GROUPS = 32
SSM_STATE = 64
POOL_WINDOWS = (2, 4, 8, 16)
FFN_RES = 0.5
EPS = 1e-6
GELU_C0 = math.sqrt(2.0 / math.pi)
GELU_C1 = 0.044715 * GELU_C0

LANES = 128
SUBLANES = 8
BF16_SUBLANES = 16
V7X_VMEM_BYTES = 64 * 1024 * 1024
FOLD = LANES // SSM_GROUP
PAIRS = SSM_GROUPS // 2
PAIR_W = 2 * LANES
STATE_W = PAIRS * PAIR_W
SSM_VREGS = SSM_WIDTH // LANES

TM = 1024
TT = 128
KROWS = TT // FOLD
PITCH = TT + SUBLANES
FF_CHUNK = 256
FFN1_PREP = 8
TAIL_PARTS = 4
POOL_TAIL = 32
VMEM_LIMIT = V7X_VMEM_BYTES * 7 // 8


def _rms(x, g):
    return x * lax.rsqrt(jnp.mean(x * x, axis=-1, keepdims=True) + EPS) * g


def _dot(a, b):
    return jnp.dot(a, b, preferred_element_type=F32)


def _rstd(x):
    return lax.rsqrt(jnp.mean(x * x, axis=-1, keepdims=True) + EPS)


def _swiglu_hidden(x, w1_ref, w3_ref, act_ref):
    xb = x.astype(BF16)
    r = jnp.broadcast_to(_rstd(x), (x.shape[0], FF_CHUNK))
    for c in range(D_FF // FF_CHUNK):
        sl = slice(c * FF_CHUNK, (c + 1) * FF_CHUNK)
        g = _dot(xb, w1_ref[:, sl]) * r
        u = _dot(xb, w3_ref[:, sl]) * r
        act_ref[:, sl] = (g * jax.nn.sigmoid(g) * u).astype(BF16)


def _swiglu(x, w1_ref, w3_ref, w2_ref, act_ref):
    _swiglu_hidden(x, w1_ref, w3_ref, act_ref)
    return _dot(act_ref[...], w2_ref[...])


def _ffn1_kernel(*refs, side_gain):
    x_ref, g_ref, w1_ref, w3_ref, w2_ref = refs[:5]
    n_side = len(side_gain)
    n_side_in = n_side + sum(side_gain)
    side_in = iter(refs[5:5 + n_side_in])
    o_ref = refs[5 + n_side_in]
    side_out = refs[6 + n_side_in:6 + n_side_in + n_side]
    w1s_ref, w3s_ref, w2s_ref, act_ref = refs[-4:]
    t = pl.program_id(0)

    @pl.when(t < FFN1_PREP)
    def _():
        r13 = pl.multiple_of(t * w1_ref.shape[0], w1_ref.shape[0])
        w1s_ref[pl.ds(r13, w1_ref.shape[0]), :] = (w1_ref[...] * g_ref[...]).astype(BF16)
        w3s_ref[pl.ds(r13, w3_ref.shape[0]), :] = (w3_ref[...] * g_ref[...]).astype(BF16)
        r2 = pl.multiple_of(t * w2_ref.shape[0], w2_ref.shape[0])
        w2s_ref[pl.ds(r2, w2_ref.shape[0]), :] = w2_ref[...].astype(BF16)

    @pl.when(t >= FFN1_PREP)
    def _():
        x = x_ref[...]
        o_ref[...] = x + FFN_RES * _swiglu(x, w1s_ref, w3s_ref, w2s_ref, act_ref)

    for has_gain, out_ref in zip(side_gain, side_out):
        w = next(side_in)[...]
        if has_gain:
            w = w * next(side_in)[...]
        out_ref[...] = w.astype(BF16)


def _ffn2_kernel(h_ref, p_ref, w1_ref, w3_ref, w2_ref, wg_ref, wp_ref, pnorm_ref, fnorm_ref,
                 o_ref, act_ref):
    emb = _rms(_dot(p_ref[...].astype(BF16), wp_ref[...]), pnorm_ref[...])
    _swiglu_hidden(h_ref[...], w1_ref, w3_ref, act_ref)
    for hh in range(TAIL_PARTS):
        rs = slice(hh * (TM // TAIL_PARTS), (hh + 1) * (TM // TAIL_PARTS))
        hr = h_ref[rs, :] + FFN_RES * _dot(act_ref[rs, :], w2_ref[...])
        gate = jax.nn.sigmoid(_dot(hr.astype(BF16), wg_ref[...]) * _rstd(hr))
        o_ref[rs, :] = _rms(hr + gate * emb[rs, :], fnorm_ref[...])


def _block_transpose(xs, lane_blk):
    xs = list(xs)
    for s in (4, 2, 1):
        upper = (lane_blk & s) != 0
        nxt = list(xs)
        for r in range(len(xs)):
            if r & s:
                continue
            a, b = xs[r], xs[r + s]
            nxt[r] = jnp.where(upper, pltpu.roll(b, SSM_GROUP * s, 1), a)
            nxt[r + s] = jnp.where(upper, b, pltpu.roll(a, LANES - SSM_GROUP * s, 1))
        xs = nxt
    return xs


def _mixer_kernel(h_ref, win_ref, btsup_ref, csup_ref, a8_ref,
                  dskip_ref, wglu_ref, bglu_ref, wpool_ref, wout_ref, o_ref,
                  state_ref, zs_ref, ys_ref, vbuf_ref, s2_ref, s4_ref, s8_ref):
    l = pl.program_id(0)
    nb = h_ref.shape[0]
    nr = TT + POOL_TAIL
    rows = nb * KROWS

    @pl.when(l == 0)
    def _():
        state_ref[...] = jnp.zeros_like(state_ref)
        vbuf_ref[:, 0:POOL_TAIL, :] = jnp.zeros((nb, POOL_TAIL, POOL_WIDTH), F32)

    @pl.when(l > 0)
    def _():
        vbuf_ref[:, 0:POOL_TAIL, :] = vbuf_ref[:, TT:nr, :]

    h = h_ref[...].reshape(nb * TT, D_MODEL)
    z = _dot(h.astype(BF16), win_ref[...]) * _rstd(h)

    for b in range(nb):
        for v in range(SSM_VREGS):
            zs_ref[v, b * PITCH:b * PITCH + TT, :] = z[b * TT:(b + 1) * TT, v * LANES:(v + 1) * LANES]
        vbuf_ref[b, POOL_TAIL:nr, :] = z[b * TT:(b + 1) * TT, SSM_WIDTH:]

    lane_blk = lax.broadcasted_iota(jnp.int32, (rows, LANES), 1) // SSM_GROUP
    for v in range(SSM_VREGS):
        toks = [jnp.concatenate([zs_ref[v, pl.ds(FOLD * k + j, nb, stride=PITCH), :]
                                 for k in range(KROWS)], axis=0) for j in range(FOLD)]
        folded = _block_transpose(toks, lane_blk)
        ysup = []
        for pp in range(FOLD // 2):
            pair = v * (FOLD // 2) + pp
            base = pair * PAIR_W
            usup = jnp.concatenate([folded[2 * pp], folded[2 * pp + 1]], axis=1).astype(BF16)
            xy = _dot(usup, btsup_ref[pair])
            ar = jnp.broadcast_to(a8_ref[0:1, base:base + LANES], (nb, LANES))
            ai = jnp.broadcast_to(a8_ref[0:1, base + LANES:base + PAIR_W], (nb, LANES))
            sr = state_ref[:, base:base + LANES]
            si = state_ref[:, base + LANES:base + PAIR_W]
            prev_r, prev_i = [], []
            for k in range(KROWS):
                prev_r.append(sr)
                prev_i.append(si)
                xr = xy[k * nb:(k + 1) * nb, 0:LANES]
                xi = xy[k * nb:(k + 1) * nb, LANES:PAIR_W]
                sr, si = ar * sr - ai * si + xr, ar * si + ai * sr + xi
            state_ref[:, base:base + LANES] = sr
            state_ref[:, base + LANES:base + PAIR_W] = si
            sprev = jnp.concatenate([jnp.concatenate(prev_r, axis=0),
                                     jnp.concatenate(prev_i, axis=0)], axis=1).astype(BF16)
            y = xy[:, PAIR_W:] + _dot(sprev, csup_ref[pair])
            ysup.append(y[:, :LANES])
            ysup.append(y[:, LANES:])
        outs = _block_transpose(ysup, lane_blk)
        for j in range(FOLD):
            for k in range(KROWS):
                ys_ref[v, pl.ds(FOLD * k + j, nb, stride=PITCH), :] = outs[j][k * nb:(k + 1) * nb, :]

    y = jnp.concatenate(
        [jnp.concatenate([ys_ref[v, b * PITCH:b * PITCH + TT, :] for v in range(SSM_VREGS)], axis=1)
         for b in range(nb)], axis=0)
    y = y + dskip_ref[...] * z[:, :SSM_WIDTH]
    y = y * (0.5 + 0.5 * jnp.tanh(y * (GELU_C0 + GELU_C1 * (y * y))))
    y = y * jax.nn.sigmoid(_dot(y.astype(BF16), wglu_ref[...]) + bglu_ref[...])

    s2_ref[:, 8:nr, :] = vbuf_ref[:, 8:nr, :] + vbuf_ref[:, 7:nr - 1, :]
    s4_ref[:, 16:nr, :] = s2_ref[:, 16:nr, LANES:] + s2_ref[:, 14:nr - 2, LANES:]
    s8_ref[:, 24:nr, :] = s4_ref[:, 24:nr, LANES:] + s4_ref[:, 20:nr - 4, LANES:]
    s16 = s8_ref[:, 32:nr, LANES:] + s8_ref[:, 24:nr - 8, LANES:]
    wins = [s2_ref[:, POOL_TAIL:nr, 0:LANES], s4_ref[:, POOL_TAIL:nr, 0:LANES],
            s8_ref[:, POOL_TAIL:nr, 0:LANES], s16]
    t = l * TT + lax.broadcasted_iota(jnp.int32, (1, TT, 1), 1)
    pooled = []
    for gi, w in enumerate(POOL_WINDOWS):
        inv_count = 1.0 / jnp.minimum(t + 1, w).astype(F32)
        vg = vbuf_ref[:, POOL_TAIL:nr, gi * LANES:(gi + 1) * LANES]
        pg = (wins[gi] * inv_count - vg).reshape(nb * TT, LANES).astype(BF16)
        pooled.append(_dot(pg, wpool_ref[gi]))
    y_pool = jnp.concatenate(pooled, axis=1)

    yn = jnp.concatenate([(y * _rstd(y)).astype(BF16), (y_pool * _rstd(y_pool)).astype(BF16)], axis=1)
    o_ref[...] = (h + _dot(yn, wout_ref[...])).reshape(nb, TT, D_MODEL)


def _cmul(xr, xi, yr, yi):
    return xr * yr - xi * yi, xr * yi + xi * yr


def _dot_nt(a, b):
    return lax.dot_general(a, b, (((1,), (1,)), ((), ())), precision=lax.Precision.HIGHEST,
                           preferred_element_type=F32)


def _tables_kernel(lam_ref, bt_ref, ct_ref, btsup_ref, csup_ref, a8_ref, pwr_ref, pwi_ref):
    lam_r, lam_i, dt = lam_ref[0:1, :], lam_ref[1:2, :], jnp.exp(lam_ref[2:3, :])
    m = lax.broadcasted_iota(jnp.int32, (2 * FOLD, LANES), 0).astype(F32)
    mag = jnp.exp(m * (lam_r * dt))
    pwr_ref[...] = mag * jnp.cos(m * (lam_i * dt))
    pwi_ref[...] = mag * jnp.sin(m * (lam_i * dt))
    ar, ai = pwr_ref[1:2, :], pwi_ref[1:2, :]
    den = lam_r * lam_r + lam_i * lam_i
    nr = ar - 1.0
    f_r = (nr * lam_r + ai * lam_i) / den
    f_i = (ai * lam_r - nr * lam_i) / den
    bb_r, bb_i = _cmul(f_r, f_i, bt_ref[0], bt_ref[1])
    c_r, c_i = ct_ref[0], ct_ref[1]

    def tiled(x):
        return jnp.concatenate([x] * FOLD, axis=0)

    def power_rows(first, step):
        ks = [first + step * j for j in range(FOLD)]
        return (jnp.concatenate([jnp.broadcast_to(pwr_ref[k:k + 1, :], (SSM_GROUP, LANES)) for k in ks], axis=0),
                jnp.concatenate([jnp.broadcast_to(pwi_ref[k:k + 1, :], (SSM_GROUP, LANES)) for k in ks], axis=0))

    lane = lax.broadcasted_iota(jnp.int32, (LANES, LANES), 1)
    row = lax.broadcasted_iota(jnp.int32, (LANES, LANES), 0)
    g0_lanes = lane < SSM_STATE
    g0_rows = row < SSM_STATE
    zero = jnp.zeros((LANES, LANES), F32)

    s_r, s_i = _cmul(tiled(bb_r), tiled(bb_i), *power_rows(FOLD - 1, -1))
    btsup_ref[0:LANES, 0:LANES] = jnp.where(g0_lanes, s_r, zero).astype(BF16)
    btsup_ref[0:LANES, LANES:PAIR_W] = jnp.where(g0_lanes, s_i, zero).astype(BF16)
    btsup_ref[LANES:PAIR_W, 0:LANES] = jnp.where(g0_lanes, zero, s_r).astype(BF16)
    btsup_ref[LANES:PAIR_W, LANES:PAIR_W] = jnp.where(g0_lanes, zero, s_i).astype(BF16)

    m_r, m_i = _cmul(tiled(c_r), tiled(c_i), *power_rows(1, 1))
    mt_r, mt_i = m_r.T, -m_i.T
    csup_ref[0:LANES, 0:LANES] = jnp.where(g0_rows, mt_r, zero).astype(BF16)
    csup_ref[0:LANES, LANES:PAIR_W] = jnp.where(g0_rows, zero, mt_r).astype(BF16)
    csup_ref[LANES:PAIR_W, 0:LANES] = jnp.where(g0_rows, mt_i, zero).astype(BF16)
    csup_ref[LANES:PAIR_W, LANES:PAIR_W] = jnp.where(g0_rows, zero, mt_i).astype(BF16)

    w_r, w_i = _cmul(tiled(c_r), tiled(c_i), *power_rows(0, 1))
    lane16 = lax.broadcasted_iota(jnp.int32, (SSM_GROUP, LANES), 1)
    for gl in range(2):
        mine = (lane16 < SSM_STATE) if gl == 0 else (lane16 >= SSM_STATE)
        kern = (_dot_nt(jnp.where(mine, bb_r, 0.0), w_r) - _dot_nt(jnp.where(mine, bb_i, 0.0), w_i))
        toe = jnp.concatenate(
            [jnp.where(lane16 >= SSM_GROUP * i, pltpu.roll(kern, SSM_GROUP * i, 1), 0.0) if i else kern
             for i in range(FOLD)], axis=0)
        rows = slice(gl * LANES, (gl + 1) * LANES)
        btsup_ref[rows, PAIR_W + gl * LANES:PAIR_W + (gl + 1) * LANES] = toe.astype(BF16)
        btsup_ref[rows, PAIR_W + (1 - gl) * LANES:PAIR_W + (2 - gl) * LANES] = jnp.zeros((LANES, LANES), BF16)

    a8_ref[...] = jnp.concatenate([pwr_ref[FOLD:FOLD + 1, :], pwi_ref[FOLD:FOLD + 1, :]], axis=1)


def _ssm_tables(a_re, a_im, log_dt, b_re, b_im, c_re, c_im):
    lanes_gn = lambda a: a.astype(F32).reshape(PAIRS, 1, LANES)
    lam = jnp.concatenate(
        [lanes_gn(a_re), lanes_gn(a_im),
         lanes_gn(jnp.broadcast_to(log_dt[:, None], (SSM_GROUPS, SSM_STATE))),
         jnp.zeros((PAIRS, SUBLANES - 3, LANES), F32)], axis=1)
    bt = jnp.stack([b_re, b_im]).astype(F32).reshape(2, PAIRS, 2, SSM_STATE, SSM_GROUP)
    bt = jnp.transpose(bt, (1, 0, 4, 2, 3)).reshape(PAIRS, 2, SSM_GROUP, LANES)
    ct = jnp.stack([c_re, c_im]).astype(F32).reshape(2, PAIRS, 2, SSM_GROUP, SSM_STATE)
    ct = jnp.transpose(ct, (1, 0, 3, 2, 4)).reshape(PAIRS, 2, SSM_GROUP, LANES)
    return pl.pallas_call(
        _tables_kernel,
        grid=(PAIRS,),
        in_specs=[pl.BlockSpec((None, SUBLANES, LANES), lambda q: (q, 0, 0)),
                  pl.BlockSpec((None, 2, SSM_GROUP, LANES), lambda q: (q, 0, 0, 0)),
                  pl.BlockSpec((None, 2, SSM_GROUP, LANES), lambda q: (q, 0, 0, 0))],
        out_specs=[pl.BlockSpec((None, PAIR_W, 2 * PAIR_W), lambda q: (q, 0, 0)),
                   pl.BlockSpec((None, PAIR_W, PAIR_W), lambda q: (q, 0, 0)),
                   pl.BlockSpec((1, PAIR_W), lambda q: (0, q))],
        out_shape=[jax.ShapeDtypeStruct((PAIRS, PAIR_W, 2 * PAIR_W), BF16),
                   jax.ShapeDtypeStruct((PAIRS, PAIR_W, PAIR_W), BF16),
                   jax.ShapeDtypeStruct((1, STATE_W), F32)],
        scratch_shapes=[pltpu.VMEM((2 * FOLD, LANES), F32), pltpu.VMEM((2 * FOLD, LANES), F32)],
        compiler_params=pltpu.CompilerParams(dimension_semantics=("arbitrary",)),
        name="s5_tables",
    )(lam, bt, ct)


def _const_spec(shape):
    nd = len(shape)
    return pl.BlockSpec(shape, lambda *_: (0,) * nd)


def _params():
    return pltpu.CompilerParams(dimension_semantics=("arbitrary",), vmem_limit_bytes=VMEM_LIMIT)


def kernel(x, p, ffn1_norm, ffn1_w1, ffn1_w3, ffn1_w2, mix_norm, w_in, a_re, a_im, log_dt, b_re, b_im,
           c_re, c_im, d_skip, w_glu, b_glu, w_pool, pool_scale, ssm_out_norm, pool_out_norm, w_out,
           ffn2_norm, ffn2_w1, ffn2_w3, ffn2_w2, ple_gate_norm, w_ple_gate, w_ple_proj, ple_norm,
           final_norm):
    bsz, seqlen, _ = x.shape
    depth = ffn1_w1.shape[0]
    n_tok = bsz * seqlen
    assert n_tok % TM == 0 and seqlen % TT == 0 and bsz == SUBLANES

    row = lambda a: a.reshape(1, -1).astype(F32)
    tok_spec = pl.BlockSpec((TM, D_MODEL), lambda t: (t, 0))
    seq_spec = pl.BlockSpec((bsz, TT, D_MODEL), lambda l: (0, l, 0))

    assert depth == 1
    i = 0

    out_gain = jnp.concatenate([ssm_out_norm[i], pool_out_norm[i]])
    side = [(ffn2_w1[i], ffn2_norm[i]), (ffn2_w3[i], ffn2_norm[i]), (ffn2_w2[i], None),
            (w_ple_gate[i], ple_gate_norm[i]), (w_ple_proj[i], None),
            (w_in[i], mix_norm[i]), (w_out[i], out_gain), (w_glu[i], None)]
    n_steps = n_tok // TM + FFN1_PREP
    side_args, side_in_specs, side_out_specs, side_out_shapes = [], [], [], []
    for w, g in side:
        rows, cols = w.shape
        units = rows // BF16_SUBLANES
        n_blk = max(d for d in range(1, n_steps + 1) if units % d == 0)
        blk = rows // n_blk
        imap = functools.partial(lambda t, last: (jnp.minimum(t, last), 0), last=n_blk - 1)
        side_args.append(w.astype(F32))
        side_in_specs.append(pl.BlockSpec((blk, cols), imap))
        if g is not None:
            side_args.append(g.astype(F32).reshape(rows, 1))
            side_in_specs.append(pl.BlockSpec((blk, 1), imap))
        side_out_specs.append(pl.BlockSpec((blk, cols), imap))
        side_out_shapes.append(jax.ShapeDtypeStruct((rows, cols), BF16))

    prep_map = lambda t: (jnp.minimum(t, FFN1_PREP - 1), 0)
    ffn1_tok_spec = pl.BlockSpec((TM, D_MODEL), lambda t: (jnp.maximum(t - FFN1_PREP, 0), 0))
    h, w1b, w3b, w2b, wgb, wpb, winb, woutb, wglub = pl.pallas_call(
        functools.partial(_ffn1_kernel, side_gain=tuple(g is not None for _, g in side)),
        grid=(n_steps,),
        in_specs=[ffn1_tok_spec, pl.BlockSpec((D_MODEL // FFN1_PREP, 1), prep_map),
                  pl.BlockSpec((D_MODEL // FFN1_PREP, D_FF), prep_map),
                  pl.BlockSpec((D_MODEL // FFN1_PREP, D_FF), prep_map),
                  pl.BlockSpec((D_FF // FFN1_PREP, D_MODEL), prep_map)] + side_in_specs,
        out_specs=[ffn1_tok_spec] + side_out_specs,
        out_shape=[jax.ShapeDtypeStruct((n_tok, D_MODEL), F32)] + side_out_shapes,
        scratch_shapes=[pltpu.VMEM((D_MODEL, D_FF), BF16), pltpu.VMEM((D_MODEL, D_FF), BF16),
                        pltpu.VMEM((D_FF, D_MODEL), BF16), pltpu.VMEM((TM, D_FF), BF16)],
        compiler_params=_params(),
        name="ffn1",
    )(x.reshape(n_tok, D_MODEL), ffn1_norm[i].astype(F32).reshape(D_MODEL, 1), ffn1_w1[i].astype(F32),
      ffn1_w3[i].astype(F32), ffn1_w2[i].astype(F32), *side_args)

    btsup, csup, a8 = _ssm_tables(a_re[i], a_im[i], log_dt[i], b_re[i], b_im[i], c_re[i], c_im[i])
    pool_cols = pool_scale[i].astype(F32).reshape(len(POOL_WINDOWS), 1, LANES)
    consts = [winb, btsup, csup, a8, row(d_skip[i]), wglub, row(b_glu[i]),
              (w_pool[i].astype(F32) * pool_cols).astype(BF16), woutb]
    nr = TT + POOL_TAIL
    h = pl.pallas_call(
        _mixer_kernel,
        grid=(seqlen // TT,),
        in_specs=[seq_spec] + [_const_spec(c.shape) for c in consts],
        out_specs=seq_spec,
        out_shape=jax.ShapeDtypeStruct((bsz, seqlen, D_MODEL), F32),
        scratch_shapes=[pltpu.VMEM((bsz, STATE_W), F32),
                        pltpu.VMEM((SSM_VREGS, bsz * PITCH, LANES), F32),
                        pltpu.VMEM((SSM_VREGS, bsz * PITCH, LANES), F32),
                        pltpu.VMEM((bsz, nr, POOL_WIDTH), F32),
                        pltpu.VMEM((bsz, nr, POOL_WIDTH), F32),
                        pltpu.VMEM((bsz, nr, POOL_WIDTH - LANES), F32),
                        pltpu.VMEM((bsz, nr, POOL_WIDTH - 2 * LANES), F32)],
        compiler_params=_params(),
        name="mixer",
    )(h.reshape(bsz, seqlen, D_MODEL), *consts)

    consts = [w1b, w3b, w2b, wgb, wpb, row(ple_norm[i]), row(final_norm)]
    h = pl.pallas_call(
        _ffn2_kernel,
        grid=(n_tok // TM,),
        in_specs=[tok_spec, pl.BlockSpec((TM, PLE_DIM), lambda t: (t, 0))]
                 + [_const_spec(c.shape) for c in consts],
        out_specs=tok_spec,
        out_shape=jax.ShapeDtypeStruct((n_tok, D_MODEL), F32),
        scratch_shapes=[pltpu.VMEM((TM, D_FF), BF16)],
        compiler_params=_params(),
        name="ffn2",
    )(h.reshape(n_tok, D_MODEL), p[i].reshape(n_tok, PLE_DIM), *consts)
    return h.reshape(bsz, seqlen, D_MODEL)
```

```python
import functools
import math

import jax
import jax.numpy as jnp
from jax import lax
from jax.experimental import pallas as pl
from jax.experimental.pallas import tpu as pltpu

F32 = jnp.float32
BF16 = jnp.bfloat16

D_MODEL = 1024
D_FF = 2816
PLE_DIM = 256
SSM_WIDTH = 512
POOL_WIDTH = 512
SSM_GROUP = 16
SSM_GROUPS = 32
SSM_STATE = 64
POOL_WINDOWS = (2, 4, 8, 16)
FFN_RES = 0.5
EPS = 1e-6
GELU_C0 = math.sqrt(2.0 / math.pi)
GELU_C1 = 0.044715 * GELU_C0

LANES = 128
SUBLANES = 8
BF16_SUBLANES = 16
V7X_VMEM_BYTES = 64 * 1024 * 1024
FOLD = LANES // SSM_GROUP
PAIRS = SSM_GROUPS // 2
PAIR_W = 2 * LANES
STATE_W = PAIRS * PAIR_W
SSM_VREGS = SSM_WIDTH // LANES

TM = 1024
TT = 128
KROWS = TT // FOLD
PITCH = TT + SUBLANES
FF_CHUNK = 256
FFN1_PREP = 8
TAIL_PARTS = 4
POOL_TAIL = 32
VMEM_LIMIT = V7X_VMEM_BYTES * 7 // 8


def _rms(x, g):
    return x * lax.rsqrt(jnp.mean(x * x, axis=-1, keepdims=True) + EPS) * g


def _dot(a, b):
    return jnp.dot(a, b, preferred_element_type=F32)


def _rstd(x):
    return lax.rsqrt(jnp.mean(x * x, axis=-1, keepdims=True) + EPS)


def _swiglu_hidden(x, w1_ref, w3_ref, act_ref):
    xb = x.astype(BF16)
    r = jnp.broadcast_to(_rstd(x), (x.shape[0], FF_CHUNK))
    for c in range(D_FF // FF_CHUNK):
        sl = slice(c * FF_CHUNK, (c + 1) * FF_CHUNK)
        g = _dot(xb, w1_ref[:, sl]) * r
        u = _dot(xb, w3_ref[:, sl]) * r
        act_ref[:, sl] = (g * jax.nn.sigmoid(g) * u).astype(BF16)


def _swiglu(x, w1_ref, w3_ref, w2_ref, act_ref):
    _swiglu_hidden(x, w1_ref, w3_ref, act_ref)
    return _dot(act_ref[...], w2_ref[...])


def _ffn1_kernel(*refs, side_gain):
    x_ref, g_ref, w1_ref, w3_ref, w2_ref = refs[:5]
    n_side = len(side_gain)
    n_side_in = n_side + sum(side_gain)
    side_in = iter(refs[5:5 + n_side_in])
    o_ref = refs[5 + n_side_in]
    side_out = refs[6 + n_side_in:6 + n_side_in + n_side]
    w1s_ref, w3s_ref, w2s_ref, act_ref = refs[-4:]
    t = pl.program_id(0)

    @pl.when(t < FFN1_PREP)
    def _():
        r13 = pl.multiple_of(t * w1_ref.shape[0], w1_ref.shape[0])
        w1s_ref[pl.ds(r13, w1_ref.shape[0]), :] = (w1_ref[...] * g_ref[...]).astype(BF16)
        w3s_ref[pl.ds(r13, w3_ref.shape[0]), :] = (w3_ref[...] * g_ref[...]).astype(BF16)
        r2 = pl.multiple_of(t * w2_ref.shape[0], w2_ref.shape[0])
        w2s_ref[pl.ds(r2, w2_ref.shape[0]), :] = w2_ref[...].astype(BF16)

    @pl.when(t >= FFN1_PREP)
    def _():
        x = x_ref[...]
        o_ref[...] = x + FFN_RES * _swiglu(x, w1s_ref, w3s_ref, w2s_ref, act_ref)

    for has_gain, out_ref in zip(side_gain, side_out):
        w = next(side_in)[...]
        if has_gain:
            w = w * next(side_in)[...]
        out_ref[...] = w.astype(BF16)


def _ffn2_kernel(h_ref, p_ref, w1_ref, w3_ref, w2_ref, wg_ref, wp_ref, pnorm_ref, fnorm_ref,
                 o_ref, act_ref):
    emb = _rms(_dot(p_ref[...].astype(BF16), wp_ref[...]), pnorm_ref[...])
    _swiglu_hidden(h_ref[...], w1_ref, w3_ref, act_ref)
    for hh in range(TAIL_PARTS):
        rs = slice(hh * (TM // TAIL_PARTS), (hh + 1) * (TM // TAIL_PARTS))
        hr = h_ref[rs, :] + FFN_RES * _dot(act_ref[rs, :], w2_ref[...])
        gate = jax.nn.sigmoid(_dot(hr.astype(BF16), wg_ref[...]) * _rstd(hr))
        o_ref[rs, :] = _rms(hr + gate * emb[rs, :], fnorm_ref[...])


def _block_transpose(xs, lane_blk):
    xs = list(xs)
    for s in (4, 2, 1):
        upper = (lane_blk & s) != 0
        nxt = list(xs)
        for r in range(len(xs)):
            if r & s:
                continue
            a, b = xs[r], xs[r + s]
            nxt[r] = jnp.where(upper, pltpu.roll(b, SSM_GROUP * s, 1), a)
            nxt[r + s] = jnp.where(upper, b, pltpu.roll(a, LANES - SSM_GROUP * s, 1))
        xs = nxt
    return xs


def _mixer_kernel(h_ref, win_ref, btsup_ref, csup_ref, a8_ref,
                  dskip_ref, wglu_ref, bglu_ref, wpool_ref, wout_ref, o_ref,
                  state_ref, zs_ref, ys_ref, vbuf_ref, s2_ref, s4_ref, s8_ref):
    l = pl.program_id(0)
    nb = h_ref.shape[0]
    nr = TT + POOL_TAIL
    rows = nb * KROWS

    @pl.when(l == 0)
    def _():
        state_ref[...] = jnp.zeros_like(state_ref)
        vbuf_ref[:, 0:POOL_TAIL, :] = jnp.zeros((nb, POOL_TAIL, POOL_WIDTH), F32)

    @pl.when(l > 0)
    def _():
        vbuf_ref[:, 0:POOL_TAIL, :] = vbuf_ref[:, TT:nr, :]

    h = h_ref[...].reshape(nb * TT, D_MODEL)
    z = _dot(h.astype(BF16), win_ref[...]) * _rstd(h)

    for b in range(nb):
        for v in range(SSM_VREGS):
            zs_ref[v, b * PITCH:b * PITCH + TT, :] = z[b * TT:(b + 1) * TT, v * LANES:(v + 1) * LANES]
        vbuf_ref[b, POOL_TAIL:nr, :] = z[b * TT:(b + 1) * TT, SSM_WIDTH:]

    lane_blk = lax.broadcasted_iota(jnp.int32, (rows, LANES), 1) // SSM_GROUP
    for v in range(SSM_VREGS):
        toks = [jnp.concatenate([zs_ref[v, pl.ds(FOLD * k + j, nb, stride=PITCH), :]
                                 for k in range(KROWS)], axis=0) for j in range(FOLD)]
        folded = _block_transpose(toks, lane_blk)
        ysup = []
        for pp in range(FOLD // 2):
            pair = v * (FOLD // 2) + pp
            base = pair * PAIR_W
            usup = jnp.concatenate([folded[2 * pp], folded[2 * pp + 1]], axis=1).astype(BF16)
            xy = _dot(usup, btsup_ref[pair])
            ar = jnp.broadcast_to(a8_ref[0:1, base:base + LANES], (nb, LANES))
            ai = jnp.broadcast_to(a8_ref[0:1, base + LANES:base + PAIR_W], (nb, LANES))
            sr = state_ref[:, base:base + LANES]
            si = state_ref[:, base + LANES:base + PAIR_W]
            prev_r, prev_i = [], []
            for k in range(KROWS):
                prev_r.append(sr)
                prev_i.append(si)
                xr = xy[k * nb:(k + 1) * nb, 0:LANES]
                xi = xy[k * nb:(k + 1) * nb, LANES:PAIR_W]
                sr, si = ar * sr - ai * si + xr, ar * si + ai * sr + xi
            state_ref[:, base:base + LANES] = sr
            state_ref[:, base + LANES:base + PAIR_W] = si
            sprev = jnp.concatenate([jnp.concatenate(prev_r, axis=0),
                                     jnp.concatenate(prev_i, axis=0)], axis=1).astype(BF16)
            y = xy[:, PAIR_W:] + _dot(sprev, csup_ref[pair])
            ysup.append(y[:, :LANES])
            ysup.append(y[:, LANES:])
        outs = _block_transpose(ysup, lane_blk)
        for j in range(FOLD):
            for k in range(KROWS):
                ys_ref[v, pl.ds(FOLD * k + j, nb, stride=PITCH), :] = outs[j][k * nb:(k + 1) * nb, :]

    y = jnp.concatenate(
        [jnp.concatenate([ys_ref[v, b * PITCH:b * PITCH + TT, :] for v in range(SSM_VREGS)], axis=1)
         for b in range(nb)], axis=0)
    y = y + dskip_ref[...] * z[:, :SSM_WIDTH]
    y = y * (0.5 + 0.5 * jnp.tanh(y * (GELU_C0 + GELU_C1 * (y * y))))
    y = y * jax.nn.sigmoid(_dot(y.astype(BF16), wglu_ref[...]) + bglu_ref[...])

    s2_ref[:, 8:nr, :] = vbuf_ref[:, 8:nr, :] + vbuf_ref[:, 7:nr - 1, :]
    s4_ref[:, 16:nr, :] = s2_ref[:, 16:nr, LANES:] + s2_ref[:, 14:nr - 2, LANES:]
    s8_ref[:, 24:nr, :] = s4_ref[:, 24:nr, LANES:] + s4_ref[:, 20:nr - 4, LANES:]
    s16 = s8_ref[:, 32:nr, LANES:] + s8_ref[:, 24:nr - 8, LANES:]
    wins = [s2_ref[:, POOL_TAIL:nr, 0:LANES], s4_ref[:, POOL_TAIL:nr, 0:LANES],
            s8_ref[:, POOL_TAIL:nr, 0:LANES], s16]
    t = l * TT + lax.broadcasted_iota(jnp.int32, (1, TT, 1), 1)
    pooled = []
    for gi, w in enumerate(POOL_WINDOWS):
        inv_count = 1.0 / jnp.minimum(t + 1, w).astype(F32)
        vg = vbuf_ref[:, POOL_TAIL:nr, gi * LANES:(gi + 1) * LANES]
        pg = (wins[gi] * inv_count - vg).reshape(nb * TT, LANES).astype(BF16)
        pooled.append(_dot(pg, wpool_ref[gi]))
    y_pool = jnp.concatenate(pooled, axis=1)

    yn = jnp.concatenate([(y * _rstd(y)).astype(BF16), (y_pool * _rstd(y_pool)).astype(BF16)], axis=1)
    o_ref[...] = (h + _dot(yn, wout_ref[...])).reshape(nb, TT, D_MODEL)


def _cmul(xr, xi, yr, yi):
    return xr * yr - xi * yi, xr * yi + xi * yr


def _dot_nt(a, b):
    return lax.dot_general(a, b, (((1,), (1,)), ((), ())), precision=lax.Precision.HIGHEST,
                           preferred_element_type=F32)


def _tables_kernel(lam_ref, bt_ref, ct_ref, btsup_ref, csup_ref, a8_ref, pwr_ref, pwi_ref):
    lam_r, lam_i, dt = lam_ref[0:1, :], lam_ref[1:2, :], jnp.exp(lam_ref[2:3, :])
    m = lax.broadcasted_iota(jnp.int32, (2 * FOLD, LANES), 0).astype(F32)
    mag = jnp.exp(m * (lam_r * dt))
    pwr_ref[...] = mag * jnp.cos(m * (lam_i * dt))
    pwi_ref[...] = mag * jnp.sin(m * (lam_i * dt))
    ar, ai = pwr_ref[1:2, :], pwi_ref[1:2, :]
    den = lam_r * lam_r + lam_i * lam_i
    nr = ar - 1.0
    f_r = (nr * lam_r + ai * lam_i) / den
    f_i = (ai * lam_r - nr * lam_i) / den
    bb_r, bb_i = _cmul(f_r, f_i, bt_ref[0], bt_ref[1])
    c_r, c_i = ct_ref[0], ct_ref[1]

    def tiled(x):
        return jnp.concatenate([x] * FOLD, axis=0)

    def power_rows(first, step):
        ks = [first + step * j for j in range(FOLD)]
        return (jnp.concatenate([jnp.broadcast_to(pwr_ref[k:k + 1, :], (SSM_GROUP, LANES)) for k in ks], axis=0),
                jnp.concatenate([jnp.broadcast_to(pwi_ref[k:k + 1, :], (SSM_GROUP, LANES)) for k in ks], axis=0))

    lane = lax.broadcasted_iota(jnp.int32, (LANES, LANES), 1)
    row = lax.broadcasted_iota(jnp.int32, (LANES, LANES), 0)
    g0_lanes = lane < SSM_STATE
    g0_rows = row < SSM_STATE
    zero = jnp.zeros((LANES, LANES), F32)

    s_r, s_i = _cmul(tiled(bb_r), tiled(bb_i), *power_rows(FOLD - 1, -1))
    btsup_ref[0:LANES, 0:LANES] = jnp.where(g0_lanes, s_r, zero).astype(BF16)
    btsup_ref[0:LANES, LANES:PAIR_W] = jnp.where(g0_lanes, s_i, zero).astype(BF16)
    btsup_ref[LANES:PAIR_W, 0:LANES] = jnp.where(g0_lanes, zero, s_r).astype(BF16)
    btsup_ref[LANES:PAIR_W, LANES:PAIR_W] = jnp.where(g0_lanes, zero, s_i).astype(BF16)

    m_r, m_i = _cmul(tiled(c_r), tiled(c_i), *power_rows(1, 1))
    mt_r, mt_i = m_r.T, -m_i.T
    csup_ref[0:LANES, 0:LANES] = jnp.where(g0_rows, mt_r, zero).astype(BF16)
    csup_ref[0:LANES, LANES:PAIR_W] = jnp.where(g0_rows, zero, mt_r).astype(BF16)
    csup_ref[LANES:PAIR_W, 0:LANES] = jnp.where(g0_rows, mt_i, zero).astype(BF16)
    csup_ref[LANES:PAIR_W, LANES:PAIR_W] = jnp.where(g0_rows, zero, mt_i).astype(BF16)

    w_r, w_i = _cmul(tiled(c_r), tiled(c_i), *power_rows(0, 1))
    lane16 = lax.broadcasted_iota(jnp.int32, (SSM_GROUP, LANES), 1)
    for gl in range(2):
        mine = (lane16 < SSM_STATE) if gl == 0 else (lane16 >= SSM_STATE)
        kern = (_dot_nt(jnp.where(mine, bb_r, 0.0), w_r) - _dot_nt(jnp.where(mine, bb_i, 0.0), w_i))
        toe = jnp.concatenate(
            [jnp.where(lane16 >= SSM_GROUP * i, pltpu.roll(kern, SSM_GROUP * i, 1), 0.0) if i else kern
             for i in range(FOLD)], axis=0)
        rows = slice(gl * LANES, (gl + 1) * LANES)
        btsup_ref[rows, PAIR_W + gl * LANES:PAIR_W + (gl + 1) * LANES] = toe.astype(BF16)
        btsup_ref[rows, PAIR_W + (1 - gl) * LANES:PAIR_W + (2 - gl) * LANES] = jnp.zeros((LANES, LANES), BF16)

    a8_ref[...] = jnp.concatenate([pwr_ref[FOLD:FOLD + 1, :], pwi_ref[FOLD:FOLD + 1, :]], axis=1)


def _ssm_tables(a_re, a_im, log_dt, b_re, b_im, c_re, c_im):
    lanes_gn = lambda a: a.astype(F32).reshape(PAIRS, 1, LANES)
    lam = jnp.concatenate(
        [lanes_gn(a_re), lanes_gn(a_im),
         lanes_gn(jnp.broadcast_to(log_dt[:, None], (SSM_GROUPS, SSM_STATE))),
         jnp.zeros((PAIRS, SUBLANES - 3, LANES), F32)], axis=1)
    bt = jnp.stack([b_re, b_im]).astype(F32).reshape(2, PAIRS, 2, SSM_STATE, SSM_GROUP)
    bt = jnp.transpose(bt, (1, 0, 4, 2, 3)).reshape(PAIRS, 2, SSM_GROUP, LANES)
    ct = jnp.stack([c_re, c_im]).astype(F32).reshape(2, PAIRS, 2, SSM_GROUP, SSM_STATE)
    ct = jnp.transpose(ct, (1, 0, 3, 2, 4)).reshape(PAIRS, 2, SSM_GROUP, LANES)
    return pl.pallas_call(
        _tables_kernel,
        grid=(PAIRS,),
        in_specs=[pl.BlockSpec((None, SUBLANES, LANES), lambda q: (q, 0, 0)),
                  pl.BlockSpec((None, 2, SSM_GROUP, LANES), lambda q: (q, 0, 0, 0)),
                  pl.BlockSpec((None, 2, SSM_GROUP, LANES), lambda q: (q, 0, 0, 0))],
        out_specs=[pl.BlockSpec((None, PAIR_W, 2 * PAIR_W), lambda q: (q, 0, 0)),
                   pl.BlockSpec((None, PAIR_W, PAIR_W), lambda q: (q, 0, 0)),
                   pl.BlockSpec((1, PAIR_W), lambda q: (0, q))],
        out_shape=[jax.ShapeDtypeStruct((PAIRS, PAIR_W, 2 * PAIR_W), BF16),
                   jax.ShapeDtypeStruct((PAIRS, PAIR_W, PAIR_W), BF16),
                   jax.ShapeDtypeStruct((1, STATE_W), F32)],
        scratch_shapes=[pltpu.VMEM((2 * FOLD, LANES), F32), pltpu.VMEM((2 * FOLD, LANES), F32)],
        compiler_params=pltpu.CompilerParams(dimension_semantics=("arbitrary",)),
        name="s5_tables",
    )(lam, bt, ct)


def _const_spec(shape):
    nd = len(shape)
    return pl.BlockSpec(shape, lambda *_: (0,) * nd)


def _params():
    return pltpu.CompilerParams(dimension_semantics=("arbitrary",), vmem_limit_bytes=VMEM_LIMIT)


def kernel(x, p, ffn1_norm, ffn1_w1, ffn1_w3, ffn1_w2, mix_norm, w_in, a_re, a_im, log_dt, b_re, b_im,
           c_re, c_im, d_skip, w_glu, b_glu, w_pool, pool_scale, ssm_out_norm, pool_out_norm, w_out,
           ffn2_norm, ffn2_w1, ffn2_w3, ffn2_w2, ple_gate_norm, w_ple_gate, w_ple_proj, ple_norm,
           final_norm):
    bsz, seqlen, _ = x.shape
    depth = ffn1_w1.shape[0]
    n_tok = bsz * seqlen
    assert n_tok % TM == 0 and seqlen % TT == 0 and bsz == SUBLANES

    row = lambda a: a.reshape(1, -1).astype(F32)
    tok_spec = pl.BlockSpec((TM, D_MODEL), lambda t: (t, 0))
    seq_spec = pl.BlockSpec((bsz, TT, D_MODEL), lambda l: (0, l, 0))

    assert depth == 1
    i = 0

    gains = jnp.concatenate([ffn1_norm[i], ffn2_norm[i], ple_gate_norm[i], mix_norm[i], ssm_out_norm[i],
                             pool_out_norm[i]]).astype(F32).reshape(-1, 1)
    side = [(ffn2_w1[i], 1), (ffn2_w3[i], 1), (ffn2_w2[i], None), (w_ple_gate[i], 2), (w_ple_proj[i], None),
            (w_in[i], 3), (w_out[i], 4), (w_glu[i], None)]
    n_steps = n_tok // TM + FFN1_PREP
    side_args, side_in_specs, side_out_specs, side_out_shapes = [], [], [], []
    for w, slot in side:
        rows, cols = w.shape
        units = rows // BF16_SUBLANES
        n_blk = max(d for d in range(1, n_steps + 1) if units % d == 0)
        blk = rows // n_blk
        imap = functools.partial(lambda t, first, last: (first + jnp.minimum(t, last), 0), first=0, last=n_blk - 1)
        side_args.append(w.astype(F32))
        side_in_specs.append(pl.BlockSpec((blk, cols), imap))
        if slot is not None:
            assert rows == D_MODEL
            side_args.append(gains)
            side_in_specs.append(pl.BlockSpec((blk, 1), functools.partial(imap, first=slot * n_blk)))
        side_out_specs.append(pl.BlockSpec((blk, cols), imap))
        side_out_shapes.append(jax.ShapeDtypeStruct((rows, cols), BF16))

    prep_map = lambda t: (jnp.minimum(t, FFN1_PREP - 1), 0)
    ffn1_tok_spec = pl.BlockSpec((TM, D_MODEL), lambda t: (jnp.maximum(t - FFN1_PREP, 0), 0))
    h, w1b, w3b, w2b, wgb, wpb, winb, woutb, wglub = pl.pallas_call(
        functools.partial(_ffn1_kernel, side_gain=tuple(slot is not None for _, slot in side)),
        grid=(n_steps,),
        in_specs=[ffn1_tok_spec, pl.BlockSpec((D_MODEL // FFN1_PREP, 1), prep_map),
                  pl.BlockSpec((D_MODEL // FFN1_PREP, D_FF), prep_map),
                  pl.BlockSpec((D_MODEL // FFN1_PREP, D_FF), prep_map),
                  pl.BlockSpec((D_FF // FFN1_PREP, D_MODEL), prep_map)] + side_in_specs,
        out_specs=[ffn1_tok_spec] + side_out_specs,
        out_shape=[jax.ShapeDtypeStruct((n_tok, D_MODEL), F32)] + side_out_shapes,
        scratch_shapes=[pltpu.VMEM((D_MODEL, D_FF), BF16), pltpu.VMEM((D_MODEL, D_FF), BF16),
                        pltpu.VMEM((D_FF, D_MODEL), BF16), pltpu.VMEM((TM, D_FF), BF16)],
        compiler_params=_params(),
        name="ffn1",
    )(x.reshape(n_tok, D_MODEL), gains, ffn1_w1[i].astype(F32),
      ffn1_w3[i].astype(F32), ffn1_w2[i].astype(F32), *side_args)

    btsup, csup, a8 = _ssm_tables(a_re[i], a_im[i], log_dt[i], b_re[i], b_im[i], c_re[i], c_im[i])
    pool_cols = pool_scale[i].astype(F32).reshape(len(POOL_WINDOWS), 1, LANES)
    consts = [winb, btsup, csup, a8, row(d_skip[i]), wglub, row(b_glu[i]),
              (w_pool[i].astype(F32) * pool_cols).astype(BF16), woutb]
    nr = TT + POOL_TAIL
    h = pl.pallas_call(
        _mixer_kernel,
        grid=(seqlen // TT,),
        in_specs=[seq_spec] + [_const_spec(c.shape) for c in consts],
        out_specs=seq_spec,
        out_shape=jax.ShapeDtypeStruct((bsz, seqlen, D_MODEL), F32),
        scratch_shapes=[pltpu.VMEM((bsz, STATE_W), F32),
                        pltpu.VMEM((SSM_VREGS, bsz * PITCH, LANES), F32),
                        pltpu.VMEM((SSM_VREGS, bsz * PITCH, LANES), F32),
                        pltpu.VMEM((bsz, nr, POOL_WIDTH), F32),
                        pltpu.VMEM((bsz, nr, POOL_WIDTH), F32),
                        pltpu.VMEM((bsz, nr, POOL_WIDTH - LANES), F32),
                        pltpu.VMEM((bsz, nr, POOL_WIDTH - 2 * LANES), F32)],
        compiler_params=_params(),
        name="mixer",
    )(h.reshape(bsz, seqlen, D_MODEL), *consts)

    consts = [w1b, w3b, w2b, wgb, wpb, row(ple_norm[i]), row(final_norm)]
    h = pl.pallas_call(
        _ffn2_kernel,
        grid=(n_tok // TM,),
        in_specs=[tok_spec, pl.BlockSpec((TM, PLE_DIM), lambda t: (t, 0))]
                 + [_const_spec(c.shape) for c in consts],
        out_specs=tok_spec,
        out_shape=jax.ShapeDtypeStruct((n_tok, D_MODEL), F32),
        scratch_shapes=[pltpu.VMEM((TM, D_FF), BF16)],
        compiler_params=_params(),
        name="ffn2",
    )(h.reshape(n_tok, D_MODEL), p[i].reshape(n_tok, PLE_DIM), *consts)
    return h.reshape(bsz, seqlen, D_MODEL)
```

```python
import functools
import math

import jax
import jax.numpy as jnp
from jax import lax
from jax.experimental import pallas as pl
from jax.experimental.pallas import tpu as pltpu

F32 = jnp.float32
BF16 = jnp.bfloat16

D_MODEL = 1024
D_FF = 2816
PLE_DIM = 256
SSM_WIDTH = 512
POOL_WIDTH = 512
SSM_GROUP = 16
SSM_GROUPS = 32
SSM_STATE = 64
POOL_WINDOWS = (2, 4, 8, 16)
FFN_RES = 0.5
EPS = 1e-6
GELU_C0 = math.sqrt(2.0 / math.pi)
GELU_C1 = 0.044715 * GELU_C0

LANES = 128
SUBLANES = 8
BF16_SUBLANES = 16
V7X_VMEM_BYTES = 64 * 1024 * 1024
FOLD = LANES // SSM_GROUP
PAIRS = SSM_GROUPS // 2
PAIR_W = 2 * LANES
STATE_W = PAIRS * PAIR_W
SSM_VREGS = SSM_WIDTH // LANES

TM = 1024
TT = 128
KROWS = TT // FOLD
PITCH = TT + SUBLANES
FF_CHUNK = 256
TAIL_PARTS = 4
POOL_TAIL = 32
VMEM_LIMIT = V7X_VMEM_BYTES * 7 // 8


def _rms(x, g):
    return x * lax.rsqrt(jnp.mean(x * x, axis=-1, keepdims=True) + EPS) * g


def _dot(a, b):
    return jnp.dot(a, b, preferred_element_type=F32)


def _rstd(x):
    return lax.rsqrt(jnp.mean(x * x, axis=-1, keepdims=True) + EPS)


def _swiglu_hidden(x, w1_ref, w3_ref, act_ref):
    xb = x.astype(BF16)
    r = jnp.broadcast_to(_rstd(x), (x.shape[0], FF_CHUNK))
    for c in range(D_FF // FF_CHUNK):
        sl = slice(c * FF_CHUNK, (c + 1) * FF_CHUNK)
        g = _dot(xb, w1_ref[:, sl]) * r
        u = _dot(xb, w3_ref[:, sl]) * r
        act_ref[:, sl] = (g * jax.nn.sigmoid(g) * u).astype(BF16)


def _swiglu(x, w1_ref, w3_ref, w2_ref, act_ref):
    _swiglu_hidden(x, w1_ref, w3_ref, act_ref)
    return _dot(act_ref[...], w2_ref[...])


def _ffn1_kernel(*refs, side_gain):
    x_ref, w1_ref, w3_ref, w2_ref = refs[:4]
    n_side = len(side_gain)
    n_side_in = n_side + sum(side_gain)
    side_in = iter(refs[4:4 + n_side_in])
    o_ref = refs[4 + n_side_in]
    side_out = refs[5 + n_side_in:5 + n_side_in + n_side]
    act_ref = refs[-1]

    x = x_ref[...]
    o_ref[...] = x + FFN_RES * _swiglu(x, w1_ref, w3_ref, w2_ref, act_ref)

    for has_gain, out_ref in zip(side_gain, side_out):
        w = next(side_in)[...]
        if has_gain:
            w = w * next(side_in)[...]
        out_ref[...] = w.astype(BF16)


def _ffn2_kernel(h_ref, p_ref, w1_ref, w3_ref, w2_ref, wg_ref, wp_ref, pnorm_ref, fnorm_ref,
                 o_ref, act_ref):
    emb = _rms(_dot(p_ref[...].astype(BF16), wp_ref[...]), pnorm_ref[...])
    _swiglu_hidden(h_ref[...], w1_ref, w3_ref, act_ref)
    for hh in range(TAIL_PARTS):
        rs = slice(hh * (TM // TAIL_PARTS), (hh + 1) * (TM // TAIL_PARTS))
        hr = h_ref[rs, :] + FFN_RES * _dot(act_ref[rs, :], w2_ref[...])
        gate = jax.nn.sigmoid(_dot(hr.astype(BF16), wg_ref[...]) * _rstd(hr))
        o_ref[rs, :] = _rms(hr + gate * emb[rs, :], fnorm_ref[...])


def _block_transpose(xs, lane_blk):
    xs = list(xs)
    for s in (4, 2, 1):
        upper = (lane_blk & s) != 0
        nxt = list(xs)
        for r in range(len(xs)):
            if r & s:
                continue
            a, b = xs[r], xs[r + s]
            nxt[r] = jnp.where(upper, pltpu.roll(b, SSM_GROUP * s, 1), a)
            nxt[r + s] = jnp.where(upper, b, pltpu.roll(a, LANES - SSM_GROUP * s, 1))
        xs = nxt
    return xs


def _mixer_kernel(h_ref, win_ref, btsup_ref, csup_ref, a8_ref,
                  dskip_ref, wglu_ref, bglu_ref, wpool_ref, wout_ref, o_ref,
                  state_ref, zs_ref, ys_ref, vbuf_ref, s2_ref, s4_ref, s8_ref):
    l = pl.program_id(0)
    nb = h_ref.shape[0]
    nr = TT + POOL_TAIL
    rows = nb * KROWS

    @pl.when(l == 0)
    def _():
        state_ref[...] = jnp.zeros_like(state_ref)
        vbuf_ref[:, 0:POOL_TAIL, :] = jnp.zeros((nb, POOL_TAIL, POOL_WIDTH), F32)

    @pl.when(l > 0)
    def _():
        vbuf_ref[:, 0:POOL_TAIL, :] = vbuf_ref[:, TT:nr, :]

    h = h_ref[...].reshape(nb * TT, D_MODEL)
    z = _dot(h.astype(BF16), win_ref[...]) * _rstd(h)

    for b in range(nb):
        for v in range(SSM_VREGS):
            zs_ref[v, b * PITCH:b * PITCH + TT, :] = z[b * TT:(b + 1) * TT, v * LANES:(v + 1) * LANES]
        vbuf_ref[b, POOL_TAIL:nr, :] = z[b * TT:(b + 1) * TT, SSM_WIDTH:]

    lane_blk = lax.broadcasted_iota(jnp.int32, (rows, LANES), 1) // SSM_GROUP
    for v in range(SSM_VREGS):
        toks = [jnp.concatenate([zs_ref[v, pl.ds(FOLD * k + j, nb, stride=PITCH), :]
                                 for k in range(KROWS)], axis=0) for j in range(FOLD)]
        folded = _block_transpose(toks, lane_blk)
        ysup = []
        for pp in range(FOLD // 2):
            pair = v * (FOLD // 2) + pp
            base = pair * PAIR_W
            usup = jnp.concatenate([folded[2 * pp], folded[2 * pp + 1]], axis=1).astype(BF16)
            xy = _dot(usup, btsup_ref[pair])
            ar = jnp.broadcast_to(a8_ref[0:1, base:base + LANES], (nb, LANES))
            ai = jnp.broadcast_to(a8_ref[0:1, base + LANES:base + PAIR_W], (nb, LANES))
            sr = state_ref[:, base:base + LANES]
            si = state_ref[:, base + LANES:base + PAIR_W]
            prev_r, prev_i = [], []
            for k in range(KROWS):
                prev_r.append(sr)
                prev_i.append(si)
                xr = xy[k * nb:(k + 1) * nb, 0:LANES]
                xi = xy[k * nb:(k + 1) * nb, LANES:PAIR_W]
                sr, si = ar * sr - ai * si + xr, ar * si + ai * sr + xi
            state_ref[:, base:base + LANES] = sr
            state_ref[:, base + LANES:base + PAIR_W] = si
            sprev = jnp.concatenate([jnp.concatenate(prev_r, axis=0),
                                     jnp.concatenate(prev_i, axis=0)], axis=1).astype(BF16)
            y = xy[:, PAIR_W:] + _dot(sprev, csup_ref[pair])
            ysup.append(y[:, :LANES])
            ysup.append(y[:, LANES:])
        outs = _block_transpose(ysup, lane_blk)
        for j in range(FOLD):
            for k in range(KROWS):
                ys_ref[v, pl.ds(FOLD * k + j, nb, stride=PITCH), :] = outs[j][k * nb:(k + 1) * nb, :]

    y = jnp.concatenate(
        [jnp.concatenate([ys_ref[v, b * PITCH:b * PITCH + TT, :] for v in range(SSM_VREGS)], axis=1)
         for b in range(nb)], axis=0)
    y = y + dskip_ref[...] * z[:, :SSM_WIDTH]
    y = y * (0.5 + 0.5 * jnp.tanh(y * (GELU_C0 + GELU_C1 * (y * y))))
    y = y * jax.nn.sigmoid(_dot(y.astype(BF16), wglu_ref[...]) + bglu_ref[...])

    s2_ref[:, 8:nr, :] = vbuf_ref[:, 8:nr, :] + vbuf_ref[:, 7:nr - 1, :]
    s4_ref[:, 16:nr, :] = s2_ref[:, 16:nr, LANES:] + s2_ref[:, 14:nr - 2, LANES:]
    s8_ref[:, 24:nr, :] = s4_ref[:, 24:nr, LANES:] + s4_ref[:, 20:nr - 4, LANES:]
    s16 = s8_ref[:, 32:nr, LANES:] + s8_ref[:, 24:nr - 8, LANES:]
    wins = [s2_ref[:, POOL_TAIL:nr, 0:LANES], s4_ref[:, POOL_TAIL:nr, 0:LANES],
            s8_ref[:, POOL_TAIL:nr, 0:LANES], s16]
    t = l * TT + lax.broadcasted_iota(jnp.int32, (1, TT, 1), 1)
    pooled = []
    for gi, w in enumerate(POOL_WINDOWS):
        inv_count = 1.0 / jnp.minimum(t + 1, w).astype(F32)
        vg = vbuf_ref[:, POOL_TAIL:nr, gi * LANES:(gi + 1) * LANES]
        pg = (wins[gi] * inv_count - vg).reshape(nb * TT, LANES).astype(BF16)
        pooled.append(_dot(pg, wpool_ref[gi]))
    y_pool = jnp.concatenate(pooled, axis=1)

    yn = jnp.concatenate([(y * _rstd(y)).astype(BF16), (y_pool * _rstd(y_pool)).astype(BF16)], axis=1)
    o_ref[...] = (h + _dot(yn, wout_ref[...])).reshape(nb, TT, D_MODEL)


def _cmul(xr, xi, yr, yi):
    return xr * yr - xi * yi, xr * yi + xi * yr


def _dot_nt(a, b):
    return lax.dot_general(a, b, (((1,), (1,)), ((), ())), precision=lax.Precision.HIGHEST,
                           preferred_element_type=F32)


def _tables_kernel(lam_ref, bt_ref, ct_ref, g_ref, w1_ref, w3_ref, w2_ref,
                   btsup_ref, csup_ref, a8_ref, w1o_ref, w3o_ref, w2o_ref, pwr_ref, pwi_ref):
    w1o_ref[...] = (w1_ref[...] * g_ref[...]).astype(BF16)
    w3o_ref[...] = (w3_ref[...] * g_ref[...]).astype(BF16)
    w2o_ref[...] = w2_ref[...].astype(BF16)

    lam_r, lam_i, dt = lam_ref[0:1, :], lam_ref[1:2, :], jnp.exp(lam_ref[2:3, :])
    m = lax.broadcasted_iota(jnp.int32, (2 * FOLD, LANES), 0).astype(F32)
    mag = jnp.exp(m * (lam_r * dt))
    pwr_ref[...] = mag * jnp.cos(m * (lam_i * dt))
    pwi_ref[...] = mag * jnp.sin(m * (lam_i * dt))
    ar, ai = pwr_ref[1:2, :], pwi_ref[1:2, :]
    den = lam_r * lam_r + lam_i * lam_i
    nr = ar - 1.0
    f_r = (nr * lam_r + ai * lam_i) / den
    f_i = (ai * lam_r - nr * lam_i) / den
    bb_r, bb_i = _cmul(f_r, f_i, bt_ref[0], bt_ref[1])
    c_r, c_i = ct_ref[0], ct_ref[1]

    def tiled(x):
        return jnp.concatenate([x] * FOLD, axis=0)

    def power_rows(first, step):
        ks = [first + step * j for j in range(FOLD)]
        return (jnp.concatenate([jnp.broadcast_to(pwr_ref[k:k + 1, :], (SSM_GROUP, LANES)) for k in ks], axis=0),
                jnp.concatenate([jnp.broadcast_to(pwi_ref[k:k + 1, :], (SSM_GROUP, LANES)) for k in ks], axis=0))

    lane = lax.broadcasted_iota(jnp.int32, (LANES, LANES), 1)
    row = lax.broadcasted_iota(jnp.int32, (LANES, LANES), 0)
    g0_lanes = lane < SSM_STATE
    g0_rows = row < SSM_STATE
    zero = jnp.zeros((LANES, LANES), F32)

    s_r, s_i = _cmul(tiled(bb_r), tiled(bb_i), *power_rows(FOLD - 1, -1))
    btsup_ref[0:LANES, 0:LANES] = jnp.where(g0_lanes, s_r, zero).astype(BF16)
    btsup_ref[0:LANES, LANES:PAIR_W] = jnp.where(g0_lanes, s_i, zero).astype(BF16)
    btsup_ref[LANES:PAIR_W, 0:LANES] = jnp.where(g0_lanes, zero, s_r).astype(BF16)
    btsup_ref[LANES:PAIR_W, LANES:PAIR_W] = jnp.where(g0_lanes, zero, s_i).astype(BF16)

    m_r, m_i = _cmul(tiled(c_r), tiled(c_i), *power_rows(1, 1))
    mt_r, mt_i = m_r.T, -m_i.T
    csup_ref[0:LANES, 0:LANES] = jnp.where(g0_rows, mt_r, zero).astype(BF16)
    csup_ref[0:LANES, LANES:PAIR_W] = jnp.where(g0_rows, zero, mt_r).astype(BF16)
    csup_ref[LANES:PAIR_W, 0:LANES] = jnp.where(g0_rows, mt_i, zero).astype(BF16)
    csup_ref[LANES:PAIR_W, LANES:PAIR_W] = jnp.where(g0_rows, zero, mt_i).astype(BF16)

    w_r, w_i = _cmul(tiled(c_r), tiled(c_i), *power_rows(0, 1))
    lane16 = lax.broadcasted_iota(jnp.int32, (SSM_GROUP, LANES), 1)
    for gl in range(2):
        mine = (lane16 < SSM_STATE) if gl == 0 else (lane16 >= SSM_STATE)
        kern = (_dot_nt(jnp.where(mine, bb_r, 0.0), w_r) - _dot_nt(jnp.where(mine, bb_i, 0.0), w_i))
        toe = jnp.concatenate(
            [jnp.where(lane16 >= SSM_GROUP * i, pltpu.roll(kern, SSM_GROUP * i, 1), 0.0) if i else kern
             for i in range(FOLD)], axis=0)
        rows = slice(gl * LANES, (gl + 1) * LANES)
        btsup_ref[rows, PAIR_W + gl * LANES:PAIR_W + (gl + 1) * LANES] = toe.astype(BF16)
        btsup_ref[rows, PAIR_W + (1 - gl) * LANES:PAIR_W + (2 - gl) * LANES] = jnp.zeros((LANES, LANES), BF16)

    a8_ref[...] = jnp.concatenate([pwr_ref[FOLD:FOLD + 1, :], pwi_ref[FOLD:FOLD + 1, :]], axis=1)


def _ssm_tables(a_re, a_im, log_dt, b_re, b_im, c_re, c_im, gains, w1, w3, w2):
    lanes_gn = lambda a: a.astype(F32).reshape(PAIRS, 1, LANES)
    lam = jnp.concatenate(
        [lanes_gn(a_re), lanes_gn(a_im),
         lanes_gn(jnp.broadcast_to(log_dt[:, None], (SSM_GROUPS, SSM_STATE))),
         jnp.zeros((PAIRS, SUBLANES - 3, LANES), F32)], axis=1)
    bt = jnp.stack([b_re, b_im]).astype(F32).reshape(2, PAIRS, 2, SSM_STATE, SSM_GROUP)
    bt = jnp.transpose(bt, (1, 0, 4, 2, 3)).reshape(PAIRS, 2, SSM_GROUP, LANES)
    ct = jnp.stack([c_re, c_im]).astype(F32).reshape(2, PAIRS, 2, SSM_GROUP, SSM_STATE)
    ct = jnp.transpose(ct, (1, 0, 3, 2, 4)).reshape(PAIRS, 2, SSM_GROUP, LANES)
    rows13, rows2 = D_MODEL // PAIRS, D_FF // PAIRS
    step_rows = lambda q: (q, 0)
    return pl.pallas_call(
        _tables_kernel,
        grid=(PAIRS,),
        in_specs=[pl.BlockSpec((None, SUBLANES, LANES), lambda q: (q, 0, 0)),
                  pl.BlockSpec((None, 2, SSM_GROUP, LANES), lambda q: (q, 0, 0, 0)),
                  pl.BlockSpec((None, 2, SSM_GROUP, LANES), lambda q: (q, 0, 0, 0)),
                  pl.BlockSpec((rows13, 1), step_rows), pl.BlockSpec((rows13, D_FF), step_rows),
                  pl.BlockSpec((rows13, D_FF), step_rows), pl.BlockSpec((rows2, D_MODEL), step_rows)],
        out_specs=[pl.BlockSpec((None, PAIR_W, 2 * PAIR_W), lambda q: (q, 0, 0)),
                   pl.BlockSpec((None, PAIR_W, PAIR_W), lambda q: (q, 0, 0)),
                   pl.BlockSpec((1, PAIR_W), lambda q: (0, q)),
                   pl.BlockSpec((rows13, D_FF), step_rows), pl.BlockSpec((rows13, D_FF), step_rows),
                   pl.BlockSpec((rows2, D_MODEL), step_rows)],
        out_shape=[jax.ShapeDtypeStruct((PAIRS, PAIR_W, 2 * PAIR_W), BF16),
                   jax.ShapeDtypeStruct((PAIRS, PAIR_W, PAIR_W), BF16),
                   jax.ShapeDtypeStruct((1, STATE_W), F32),
                   jax.ShapeDtypeStruct((D_MODEL, D_FF), BF16), jax.ShapeDtypeStruct((D_MODEL, D_FF), BF16),
                   jax.ShapeDtypeStruct((D_FF, D_MODEL), BF16)],
        scratch_shapes=[pltpu.VMEM((2 * FOLD, LANES), F32), pltpu.VMEM((2 * FOLD, LANES), F32)],
        compiler_params=pltpu.CompilerParams(dimension_semantics=("arbitrary",)),
        name="s5_tables",
    )(lam, bt, ct, gains, w1.astype(F32), w3.astype(F32), w2.astype(F32))


def _const_spec(shape):
    nd = len(shape)
    return pl.BlockSpec(shape, lambda *_: (0,) * nd)


def _params():
    return pltpu.CompilerParams(dimension_semantics=("arbitrary",), vmem_limit_bytes=VMEM_LIMIT)


def kernel(x, p, ffn1_norm, ffn1_w1, ffn1_w3, ffn1_w2, mix_norm, w_in, a_re, a_im, log_dt, b_re, b_im,
           c_re, c_im, d_skip, w_glu, b_glu, w_pool, pool_scale, ssm_out_norm, pool_out_norm, w_out,
           ffn2_norm, ffn2_w1, ffn2_w3, ffn2_w2, ple_gate_norm, w_ple_gate, w_ple_proj, ple_norm,
           final_norm):
    bsz, seqlen, _ = x.shape
    depth = ffn1_w1.shape[0]
    n_tok = bsz * seqlen
    assert n_tok % TM == 0 and seqlen % TT == 0 and bsz == SUBLANES

    row = lambda a: a.reshape(1, -1).astype(F32)
    tok_spec = pl.BlockSpec((TM, D_MODEL), lambda t: (t, 0))
    seq_spec = pl.BlockSpec((bsz, TT, D_MODEL), lambda l: (0, l, 0))

    assert depth == 1
    i = 0

    gains = jnp.concatenate([ffn1_norm[i], ffn2_norm[i], ple_gate_norm[i], mix_norm[i], ssm_out_norm[i],
                             pool_out_norm[i]]).astype(F32).reshape(-1, 1)
    side = [(ffn2_w1[i], 1), (ffn2_w3[i], 1), (ffn2_w2[i], None), (w_ple_gate[i], 2), (w_ple_proj[i], None),
            (w_in[i], 3), (w_out[i], 4), (w_glu[i], None)]
    n_steps = n_tok // TM
    side_args, side_in_specs, side_out_specs, side_out_shapes = [], [], [], []
    for w, slot in side:
        rows, cols = w.shape
        units = rows // BF16_SUBLANES
        n_blk = max(d for d in range(1, n_steps + 1) if units % d == 0)
        blk = rows // n_blk
        imap = functools.partial(lambda t, first, last: (first + jnp.minimum(t, last), 0), first=0, last=n_blk - 1)
        side_args.append(w.astype(F32))
        side_in_specs.append(pl.BlockSpec((blk, cols), imap))
        if slot is not None:
            assert rows == D_MODEL
            side_args.append(gains)
            side_in_specs.append(pl.BlockSpec((blk, 1), functools.partial(imap, first=slot * n_blk)))
        side_out_specs.append(pl.BlockSpec((blk, cols), imap))
        side_out_shapes.append(jax.ShapeDtypeStruct((rows, cols), BF16))

    btsup, csup, a8, w1a, w3a, w2a = _ssm_tables(a_re[i], a_im[i], log_dt[i], b_re[i], b_im[i], c_re[i], c_im[i],
                                                 gains, ffn1_w1[i], ffn1_w3[i], ffn1_w2[i])

    h, w1b, w3b, w2b, wgb, wpb, winb, woutb, wglub = pl.pallas_call(
        functools.partial(_ffn1_kernel, side_gain=tuple(slot is not None for _, slot in side)),
        grid=(n_steps,),
        in_specs=[tok_spec, _const_spec((D_MODEL, D_FF)), _const_spec((D_MODEL, D_FF)),
                  _const_spec((D_FF, D_MODEL))] + side_in_specs,
        out_specs=[tok_spec] + side_out_specs,
        out_shape=[jax.ShapeDtypeStruct((n_tok, D_MODEL), F32)] + side_out_shapes,
        scratch_shapes=[pltpu.VMEM((TM, D_FF), BF16)],
        compiler_params=_params(),
        name="ffn1",
    )(x.reshape(n_tok, D_MODEL), w1a, w3a, w2a, *side_args)

    pool_cols = pool_scale[i].astype(F32).reshape(len(POOL_WINDOWS), 1, LANES)
    consts = [winb, btsup, csup, a8, row(d_skip[i]), wglub, row(b_glu[i]),
              (w_pool[i].astype(F32) * pool_cols).astype(BF16), woutb]
    nr = TT + POOL_TAIL
    h = pl.pallas_call(
        _mixer_kernel,
        grid=(seqlen // TT,),
        in_specs=[seq_spec] + [_const_spec(c.shape) for c in consts],
        out_specs=seq_spec,
        out_shape=jax.ShapeDtypeStruct((bsz, seqlen, D_MODEL), F32),
        scratch_shapes=[pltpu.VMEM((bsz, STATE_W), F32),
                        pltpu.VMEM((SSM_VREGS, bsz * PITCH, LANES), F32),
                        pltpu.VMEM((SSM_VREGS, bsz * PITCH, LANES), F32),
                        pltpu.VMEM((bsz, nr, POOL_WIDTH), F32),
                        pltpu.VMEM((bsz, nr, POOL_WIDTH), F32),
                        pltpu.VMEM((bsz, nr, POOL_WIDTH - LANES), F32),
                        pltpu.VMEM((bsz, nr, POOL_WIDTH - 2 * LANES), F32)],
        compiler_params=_params(),
        name="mixer",
    )(h.reshape(bsz, seqlen, D_MODEL), *consts)

    consts = [w1b, w3b, w2b, wgb, wpb, row(ple_norm[i]), row(final_norm)]
    h = pl.pallas_call(
        _ffn2_kernel,
        grid=(n_tok // TM,),
        in_specs=[tok_spec, pl.BlockSpec((TM, PLE_DIM), lambda t: (t, 0))]
                 + [_const_spec(c.shape) for c in consts],
        out_specs=tok_spec,
        out_shape=jax.ShapeDtypeStruct((n_tok, D_MODEL), F32),
        scratch_shapes=[pltpu.VMEM((TM, D_FF), BF16)],
        compiler_params=_params(),
        name="ffn2",
    )(h.reshape(n_tok, D_MODEL), p[i].reshape(n_tok, PLE_DIM), *consts)
    return h.reshape(bsz, seqlen, D_MODEL)
```

```python
import functools
import math

import jax
import jax.numpy as jnp
from jax import lax
from jax.experimental import pallas as pl
from jax.experimental.pallas import tpu as pltpu

F32 = jnp.float32
BF16 = jnp.bfloat16

D_MODEL = 1024
D_FF = 2816
PLE_DIM = 256
SSM_WIDTH = 512
POOL_WIDTH = 512
SSM_GROUP = 16
SSM_GROUPS = 32
SSM_STATE = 64
POOL_WINDOWS = (2, 4, 8, 16)
FFN_RES = 0.5
EPS = 1e-6
GELU_C0 = math.sqrt(2.0 / math.pi)
GELU_C1 = 0.044715 * GELU_C0

LANES = 128
SUBLANES = 8
BF16_SUBLANES = 16
V7X_VMEM_BYTES = 64 * 1024 * 1024
FOLD = LANES // SSM_GROUP
PAIRS = SSM_GROUPS // 2
PAIR_W = 2 * LANES
STATE_W = PAIRS * PAIR_W
SSM_VREGS = SSM_WIDTH // LANES

TM = 1024
TT = 128
KROWS = TT // FOLD
PITCH = TT + SUBLANES
FF_CHUNK = 256
TAIL_PARTS = 4
POOL_TAIL = 32
VMEM_LIMIT = V7X_VMEM_BYTES * 7 // 8


def _rms(x, g):
    return x * lax.rsqrt(jnp.mean(x * x, axis=-1, keepdims=True) + EPS) * g


def _dot(a, b):
    return jnp.dot(a, b, preferred_element_type=F32)


def _rstd(x):
    return lax.rsqrt(jnp.mean(x * x, axis=-1, keepdims=True) + EPS)


def _swiglu_hidden(x, w1_ref, w3_ref, act_ref):
    xb = x.astype(BF16)
    r = jnp.broadcast_to(_rstd(x), (x.shape[0], FF_CHUNK))
    for c in range(D_FF // FF_CHUNK):
        sl = slice(c * FF_CHUNK, (c + 1) * FF_CHUNK)
        g = _dot(xb, w1_ref[:, sl]) * r
        u = _dot(xb, w3_ref[:, sl]) * r
        act_ref[:, sl] = (g * jax.nn.sigmoid(g) * u).astype(BF16)


def _swiglu(x, w1_ref, w3_ref, w2_ref, act_ref):
    _swiglu_hidden(x, w1_ref, w3_ref, act_ref)
    return _dot(act_ref[...], w2_ref[...])


def _ffn1_kernel(*refs, side_gain):
    x_ref, w1_ref, w3_ref, w2_ref = refs[:4]
    n_side = len(side_gain)
    n_side_in = n_side + sum(side_gain)
    side_in = iter(refs[4:4 + n_side_in])
    o_ref = refs[4 + n_side_in]
    side_out = refs[5 + n_side_in:5 + n_side_in + n_side]
    act_ref = refs[-1]

    x = x_ref[...]
    o_ref[...] = x + FFN_RES * _swiglu(x, w1_ref, w3_ref, w2_ref, act_ref)

    for has_gain, out_ref in zip(side_gain, side_out):
        w = next(side_in)[...]
        if has_gain:
            w = w * next(side_in)[...]
        out_ref[...] = w.astype(BF16)


def _ffn2_kernel(h_ref, p_ref, w1_ref, w3_ref, w2_ref, wg_ref, wp_ref, pnorm_ref, fnorm_ref,
                 o_ref, act_ref):
    emb = _rms(_dot(p_ref[...].astype(BF16), wp_ref[...]), pnorm_ref[...])
    _swiglu_hidden(h_ref[...], w1_ref, w3_ref, act_ref)
    for hh in range(TAIL_PARTS):
        rs = slice(hh * (TM // TAIL_PARTS), (hh + 1) * (TM // TAIL_PARTS))
        hr = h_ref[rs, :] + FFN_RES * _dot(act_ref[rs, :], w2_ref[...])
        gate = jax.nn.sigmoid(_dot(hr.astype(BF16), wg_ref[...]) * _rstd(hr))
        o_ref[rs, :] = _rms(hr + gate * emb[rs, :], fnorm_ref[...])


def _block_transpose(xs, lane_blk):
    xs = list(xs)
    for s in (4, 2, 1):
        upper = (lane_blk & s) != 0
        nxt = list(xs)
        for r in range(len(xs)):
            if r & s:
                continue
            a, b = xs[r], xs[r + s]
            nxt[r] = jnp.where(upper, pltpu.roll(b, SSM_GROUP * s, 1), a)
            nxt[r + s] = jnp.where(upper, b, pltpu.roll(a, LANES - SSM_GROUP * s, 1))
        xs = nxt
    return xs


def _mixer_kernel(h_ref, win_ref, btsup_ref, csup_ref, a8_ref,
                  dskip_ref, wglu_ref, bglu_ref, wpool_ref, wout_ref, o_ref,
                  state_ref, zs_ref, ys_ref, vbuf_ref, s2_ref, s4_ref, s8_ref):
    l = pl.program_id(0)
    nb = h_ref.shape[0]
    nr = TT + POOL_TAIL
    rows = nb * KROWS

    @pl.when(l == 0)
    def _():
        state_ref[...] = jnp.zeros_like(state_ref)
        vbuf_ref[:, 0:POOL_TAIL, :] = jnp.zeros((nb, POOL_TAIL, POOL_WIDTH), F32)

    @pl.when(l > 0)
    def _():
        vbuf_ref[:, 0:POOL_TAIL, :] = vbuf_ref[:, TT:nr, :]

    h = h_ref[...].reshape(nb * TT, D_MODEL)
    z = _dot(h.astype(BF16), win_ref[...]) * _rstd(h)

    for b in range(nb):
        for v in range(SSM_VREGS):
            zs_ref[v, b * PITCH:b * PITCH + TT, :] = z[b * TT:(b + 1) * TT, v * LANES:(v + 1) * LANES]
        vbuf_ref[b, POOL_TAIL:nr, :] = z[b * TT:(b + 1) * TT, SSM_WIDTH:]

    lane_blk = lax.broadcasted_iota(jnp.int32, (rows, LANES), 1) // SSM_GROUP
    for v in range(SSM_VREGS):
        toks = [jnp.concatenate([zs_ref[v, pl.ds(FOLD * k + j, nb, stride=PITCH), :]
                                 for k in range(KROWS)], axis=0) for j in range(FOLD)]
        folded = _block_transpose(toks, lane_blk)
        ysup = []
        for pp in range(FOLD // 2):
            pair = v * (FOLD // 2) + pp
            base = pair * PAIR_W
            usup = jnp.concatenate([folded[2 * pp], folded[2 * pp + 1]], axis=1).astype(BF16)
            xy = _dot(usup, btsup_ref[pair])
            ar = jnp.broadcast_to(a8_ref[0:1, base:base + LANES], (nb, LANES))
            ai = jnp.broadcast_to(a8_ref[0:1, base + LANES:base + PAIR_W], (nb, LANES))
            sr = state_ref[:, base:base + LANES]
            si = state_ref[:, base + LANES:base + PAIR_W]
            prev_r, prev_i = [], []
            for k in range(KROWS):
                prev_r.append(sr)
                prev_i.append(si)
                xr = xy[k * nb:(k + 1) * nb, 0:LANES]
                xi = xy[k * nb:(k + 1) * nb, LANES:PAIR_W]
                sr, si = ar * sr - ai * si + xr, ar * si + ai * sr + xi
            state_ref[:, base:base + LANES] = sr
            state_ref[:, base + LANES:base + PAIR_W] = si
            sprev = jnp.concatenate([jnp.concatenate(prev_r, axis=0),
                                     jnp.concatenate(prev_i, axis=0)], axis=1).astype(BF16)
            y = xy[:, PAIR_W:] + _dot(sprev, csup_ref[pair])
            ysup.append(y[:, :LANES])
            ysup.append(y[:, LANES:])
        outs = _block_transpose(ysup, lane_blk)
        for j in range(FOLD):
            for k in range(KROWS):
                ys_ref[v, pl.ds(FOLD * k + j, nb, stride=PITCH), :] = outs[j][k * nb:(k + 1) * nb, :]

    y = jnp.concatenate(
        [jnp.concatenate([ys_ref[v, b * PITCH:b * PITCH + TT, :] for v in range(SSM_VREGS)], axis=1)
         for b in range(nb)], axis=0)
    y = y + dskip_ref[...] * z[:, :SSM_WIDTH]
    y = y * (0.5 + 0.5 * jnp.tanh(y * (GELU_C0 + GELU_C1 * (y * y))))
    y = y * jax.nn.sigmoid(_dot(y.astype(BF16), wglu_ref[...]) + bglu_ref[...])

    s2_ref[:, 8:nr, :] = vbuf_ref[:, 8:nr, :] + vbuf_ref[:, 7:nr - 1, :]
    s4_ref[:, 16:nr, :] = s2_ref[:, 16:nr, LANES:] + s2_ref[:, 14:nr - 2, LANES:]
    s8_ref[:, 24:nr, :] = s4_ref[:, 24:nr, LANES:] + s4_ref[:, 20:nr - 4, LANES:]
    s16 = s8_ref[:, 32:nr, LANES:] + s8_ref[:, 24:nr - 8, LANES:]
    wins = [s2_ref[:, POOL_TAIL:nr, 0:LANES], s4_ref[:, POOL_TAIL:nr, 0:LANES],
            s8_ref[:, POOL_TAIL:nr, 0:LANES], s16]
    t = l * TT + lax.broadcasted_iota(jnp.int32, (1, TT, 1), 1)
    pooled = []
    for gi, w in enumerate(POOL_WINDOWS):
        inv_count = 1.0 / jnp.minimum(t + 1, w).astype(F32)
        vg = vbuf_ref[:, POOL_TAIL:nr, gi * LANES:(gi + 1) * LANES]
        pg = (wins[gi] * inv_count - vg).reshape(nb * TT, LANES).astype(BF16)
        pooled.append(_dot(pg, wpool_ref[gi]))
    y_pool = jnp.concatenate(pooled, axis=1)

    yn = jnp.concatenate([(y * _rstd(y)).astype(BF16), (y_pool * _rstd(y_pool)).astype(BF16)], axis=1)
    o_ref[...] = (h + _dot(yn, wout_ref[...])).reshape(nb, TT, D_MODEL)


def _cmul(xr, xi, yr, yi):
    return xr * yr - xi * yi, xr * yi + xi * yr


def _dot_nt(a, b):
    return lax.dot_general(a, b, (((1,), (1,)), ((), ())), precision=lax.Precision.HIGHEST,
                           preferred_element_type=F32)


def _tables_kernel(lam_ref, bre_ref, bim_ref, cre_ref, cim_ref, g_ref, w1_ref, w3_ref, w2_ref,
                   btsup_ref, csup_ref, a8_ref, w1o_ref, w3o_ref, w2o_ref, pwr_ref, pwi_ref):
    w1o_ref[...] = (w1_ref[...] * g_ref[...]).astype(BF16)
    w3o_ref[...] = (w3_ref[...] * g_ref[...]).astype(BF16)
    w2o_ref[...] = w2_ref[...].astype(BF16)

    lam_r, lam_i, dt = lam_ref[0:1, :], lam_ref[1:2, :], jnp.exp(lam_ref[2:3, :])
    m = lax.broadcasted_iota(jnp.int32, (2 * FOLD, LANES), 0).astype(F32)
    mag = jnp.exp(m * (lam_r * dt))
    pwr_ref[...] = mag * jnp.cos(m * (lam_i * dt))
    pwi_ref[...] = mag * jnp.sin(m * (lam_i * dt))
    ar, ai = pwr_ref[1:2, :], pwi_ref[1:2, :]
    den = lam_r * lam_r + lam_i * lam_i
    nr = ar - 1.0
    f_r = (nr * lam_r + ai * lam_i) / den
    f_i = (ai * lam_r - nr * lam_i) / den
    eye = (lax.broadcasted_iota(jnp.int32, (SSM_GROUP, SSM_GROUP), 0)
           == lax.broadcasted_iota(jnp.int32, (SSM_GROUP, SSM_GROUP), 1)).astype(F32)
    rows_p = lambda ref: jnp.concatenate([_dot_nt(eye, ref[0]), _dot_nt(eye, ref[1])], axis=1)
    bb_r, bb_i = _cmul(f_r, f_i, rows_p(bre_ref), rows_p(bim_ref))
    c_r = jnp.concatenate([cre_ref[0], cre_ref[1]], axis=1)
    c_i = jnp.concatenate([cim_ref[0], cim_ref[1]], axis=1)

    def tiled(x):
        return jnp.concatenate([x] * FOLD, axis=0)

    def power_rows(first, step):
        ks = [first + step * j for j in range(FOLD)]
        return (jnp.concatenate([jnp.broadcast_to(pwr_ref[k:k + 1, :], (SSM_GROUP, LANES)) for k in ks], axis=0),
                jnp.concatenate([jnp.broadcast_to(pwi_ref[k:k + 1, :], (SSM_GROUP, LANES)) for k in ks], axis=0))

    lane = lax.broadcasted_iota(jnp.int32, (LANES, LANES), 1)
    row = lax.broadcasted_iota(jnp.int32, (LANES, LANES), 0)
    g0_lanes = lane < SSM_STATE
    g0_rows = row < SSM_STATE
    zero = jnp.zeros((LANES, LANES), F32)

    s_r, s_i = _cmul(tiled(bb_r), tiled(bb_i), *power_rows(FOLD - 1, -1))
    btsup_ref[0:LANES, 0:LANES] = jnp.where(g0_lanes, s_r, zero).astype(BF16)
    btsup_ref[0:LANES, LANES:PAIR_W] = jnp.where(g0_lanes, s_i, zero).astype(BF16)
    btsup_ref[LANES:PAIR_W, 0:LANES] = jnp.where(g0_lanes, zero, s_r).astype(BF16)
    btsup_ref[LANES:PAIR_W, LANES:PAIR_W] = jnp.where(g0_lanes, zero, s_i).astype(BF16)

    m_r, m_i = _cmul(tiled(c_r), tiled(c_i), *power_rows(1, 1))
    mt_r, mt_i = m_r.T, -m_i.T
    csup_ref[0:LANES, 0:LANES] = jnp.where(g0_rows, mt_r, zero).astype(BF16)
    csup_ref[0:LANES, LANES:PAIR_W] = jnp.where(g0_rows, zero, mt_r).astype(BF16)
    csup_ref[LANES:PAIR_W, 0:LANES] = jnp.where(g0_rows, mt_i, zero).astype(BF16)
    csup_ref[LANES:PAIR_W, LANES:PAIR_W] = jnp.where(g0_rows, zero, mt_i).astype(BF16)

    w_r, w_i = _cmul(tiled(c_r), tiled(c_i), *power_rows(0, 1))
    lane16 = lax.broadcasted_iota(jnp.int32, (SSM_GROUP, LANES), 1)
    for gl in range(2):
        mine = (lane16 < SSM_STATE) if gl == 0 else (lane16 >= SSM_STATE)
        kern = (_dot_nt(jnp.where(mine, bb_r, 0.0), w_r) - _dot_nt(jnp.where(mine, bb_i, 0.0), w_i))
        toe = jnp.concatenate(
            [jnp.where(lane16 >= SSM_GROUP * i, pltpu.roll(kern, SSM_GROUP * i, 1), 0.0) if i else kern
             for i in range(FOLD)], axis=0)
        rows = slice(gl * LANES, (gl + 1) * LANES)
        btsup_ref[rows, PAIR_W + gl * LANES:PAIR_W + (gl + 1) * LANES] = toe.astype(BF16)
        btsup_ref[rows, PAIR_W + (1 - gl) * LANES:PAIR_W + (2 - gl) * LANES] = jnp.zeros((LANES, LANES), BF16)

    a8_ref[...] = jnp.concatenate([pwr_ref[FOLD:FOLD + 1, :], pwi_ref[FOLD:FOLD + 1, :]], axis=1)


def _ssm_tables(a_re, a_im, log_dt, b_re, b_im, c_re, c_im, gains, w1, w3, w2):
    lanes_gn = lambda a: a.astype(F32).reshape(PAIRS, 1, LANES)
    lam = jnp.concatenate(
        [lanes_gn(a_re), lanes_gn(a_im),
         lanes_gn(jnp.broadcast_to(log_dt[:, None], (SSM_GROUPS, SSM_STATE))),
         jnp.zeros((PAIRS, SUBLANES - 3, LANES), F32)], axis=1)
    pair_b = lambda a: a.astype(F32).reshape(PAIRS, 2, SSM_STATE, SSM_GROUP)
    pair_c = lambda a: a.astype(F32).reshape(PAIRS, 2, SSM_GROUP, SSM_STATE)
    rows13, rows2 = D_MODEL // PAIRS, D_FF // PAIRS
    step_rows = lambda q: (q, 0)
    return pl.pallas_call(
        _tables_kernel,
        grid=(PAIRS,),
        in_specs=[pl.BlockSpec((None, SUBLANES, LANES), lambda q: (q, 0, 0)),
                  pl.BlockSpec((None, 2, SSM_STATE, SSM_GROUP), lambda q: (q, 0, 0, 0)),
                  pl.BlockSpec((None, 2, SSM_STATE, SSM_GROUP), lambda q: (q, 0, 0, 0)),
                  pl.BlockSpec((None, 2, SSM_GROUP, SSM_STATE), lambda q: (q, 0, 0, 0)),
                  pl.BlockSpec((None, 2, SSM_GROUP, SSM_STATE), lambda q: (q, 0, 0, 0)),
                  pl.BlockSpec((rows13, 1), step_rows), pl.BlockSpec((rows13, D_FF), step_rows),
                  pl.BlockSpec((rows13, D_FF), step_rows), pl.BlockSpec((rows2, D_MODEL), step_rows)],
        out_specs=[pl.BlockSpec((None, PAIR_W, 2 * PAIR_W), lambda q: (q, 0, 0)),
                   pl.BlockSpec((None, PAIR_W, PAIR_W), lambda q: (q, 0, 0)),
                   pl.BlockSpec((1, PAIR_W), lambda q: (0, q)),
                   pl.BlockSpec((rows13, D_FF), step_rows), pl.BlockSpec((rows13, D_FF), step_rows),
                   pl.BlockSpec((rows2, D_MODEL), step_rows)],
        out_shape=[jax.ShapeDtypeStruct((PAIRS, PAIR_W, 2 * PAIR_W), BF16),
                   jax.ShapeDtypeStruct((PAIRS, PAIR_W, PAIR_W), BF16),
                   jax.ShapeDtypeStruct((1, STATE_W), F32),
                   jax.ShapeDtypeStruct((D_MODEL, D_FF), BF16), jax.ShapeDtypeStruct((D_MODEL, D_FF), BF16),
                   jax.ShapeDtypeStruct((D_FF, D_MODEL), BF16)],
        scratch_shapes=[pltpu.VMEM((2 * FOLD, LANES), F32), pltpu.VMEM((2 * FOLD, LANES), F32)],
        compiler_params=pltpu.CompilerParams(dimension_semantics=("arbitrary",)),
        name="s5_tables",
    )(lam, pair_b(b_re), pair_b(b_im), pair_c(c_re), pair_c(c_im), gains, w1.astype(F32), w3.astype(F32), w2.astype(F32))


def _const_spec(shape):
    nd = len(shape)
    return pl.BlockSpec(shape, lambda *_: (0,) * nd)


def _params():
    return pltpu.CompilerParams(dimension_semantics=("arbitrary",), vmem_limit_bytes=VMEM_LIMIT)


def kernel(x, p, ffn1_norm, ffn1_w1, ffn1_w3, ffn1_w2, mix_norm, w_in, a_re, a_im, log_dt, b_re, b_im,
           c_re, c_im, d_skip, w_glu, b_glu, w_pool, pool_scale, ssm_out_norm, pool_out_norm, w_out,
           ffn2_norm, ffn2_w1, ffn2_w3, ffn2_w2, ple_gate_norm, w_ple_gate, w_ple_proj, ple_norm,
           final_norm):
    bsz, seqlen, _ = x.shape
    depth = ffn1_w1.shape[0]
    n_tok = bsz * seqlen
    assert n_tok % TM == 0 and seqlen % TT == 0 and bsz == SUBLANES

    row = lambda a: a.reshape(1, -1).astype(F32)
    tok_spec = pl.BlockSpec((TM, D_MODEL), lambda t: (t, 0))
    seq_spec = pl.BlockSpec((bsz, TT, D_MODEL), lambda l: (0, l, 0))

    assert depth == 1
    i = 0

    gains = jnp.concatenate([ffn1_norm[i], ffn2_norm[i], ple_gate_norm[i], mix_norm[i], ssm_out_norm[i],
                             pool_out_norm[i]]).astype(F32).reshape(-1, 1)
    side = [(ffn2_w1[i], 1), (ffn2_w3[i], 1), (ffn2_w2[i], None), (w_ple_gate[i], 2), (w_ple_proj[i], None),
            (w_in[i], 3), (w_out[i], 4), (w_glu[i], None)]
    n_steps = n_tok // TM
    side_args, side_in_specs, side_out_specs, side_out_shapes = [], [], [], []
    for w, slot in side:
        rows, cols = w.shape
        units = rows // BF16_SUBLANES
        n_blk = max(d for d in range(1, n_steps + 1) if units % d == 0)
        blk = rows // n_blk
        imap = functools.partial(lambda t, first, last: (first + jnp.minimum(t, last), 0), first=0, last=n_blk - 1)
        side_args.append(w.astype(F32))
        side_in_specs.append(pl.BlockSpec((blk, cols), imap))
        if slot is not None:
            assert rows == D_MODEL
            side_args.append(gains)
            side_in_specs.append(pl.BlockSpec((blk, 1), functools.partial(imap, first=slot * n_blk)))
        side_out_specs.append(pl.BlockSpec((blk, cols), imap))
        side_out_shapes.append(jax.ShapeDtypeStruct((rows, cols), BF16))

    btsup, csup, a8, w1a, w3a, w2a = _ssm_tables(a_re[i], a_im[i], log_dt[i], b_re[i], b_im[i], c_re[i], c_im[i],
                                                 gains, ffn1_w1[i], ffn1_w3[i], ffn1_w2[i])

    h, w1b, w3b, w2b, wgb, wpb, winb, woutb, wglub = pl.pallas_call(
        functools.partial(_ffn1_kernel, side_gain=tuple(slot is not None for _, slot in side)),
        grid=(n_steps,),
        in_specs=[tok_spec, _const_spec((D_MODEL, D_FF)), _const_spec((D_MODEL, D_FF)),
                  _const_spec((D_FF, D_MODEL))] + side_in_specs,
        out_specs=[tok_spec] + side_out_specs,
        out_shape=[jax.ShapeDtypeStruct((n_tok, D_MODEL), F32)] + side_out_shapes,
        scratch_shapes=[pltpu.VMEM((TM, D_FF), BF16)],
        compiler_params=_params(),
        name="ffn1",
    )(x.reshape(n_tok, D_MODEL), w1a, w3a, w2a, *side_args)

    pool_cols = pool_scale[i].astype(F32).reshape(len(POOL_WINDOWS), 1, LANES)
    consts = [winb, btsup, csup, a8, row(d_skip[i]), wglub, row(b_glu[i]),
              (w_pool[i].astype(F32) * pool_cols).astype(BF16), woutb]
    nr = TT + POOL_TAIL
    h = pl.pallas_call(
        _mixer_kernel,
        grid=(seqlen // TT,),
        in_specs=[seq_spec] + [_const_spec(c.shape) for c in consts],
        out_specs=seq_spec,
        out_shape=jax.ShapeDtypeStruct((bsz, seqlen, D_MODEL), F32),
        scratch_shapes=[pltpu.VMEM((bsz, STATE_W), F32),
                        pltpu.VMEM((SSM_VREGS, bsz * PITCH, LANES), F32),
                        pltpu.VMEM((SSM_VREGS, bsz * PITCH, LANES), F32),
                        pltpu.VMEM((bsz, nr, POOL_WIDTH), F32),
                        pltpu.VMEM((bsz, nr, POOL_WIDTH), F32),
                        pltpu.VMEM((bsz, nr, POOL_WIDTH - LANES), F32),
                        pltpu.VMEM((bsz, nr, POOL_WIDTH - 2 * LANES), F32)],
        compiler_params=_params(),
        name="mixer",
    )(h.reshape(bsz, seqlen, D_MODEL), *consts)

    consts = [w1b, w3b, w2b, wgb, wpb, row(ple_norm[i]), row(final_norm)]
    h = pl.pallas_call(
        _ffn2_kernel,
        grid=(n_tok // TM,),
        in_specs=[tok_spec, pl.BlockSpec((TM, PLE_DIM), lambda t: (t, 0))]
                 + [_const_spec(c.shape) for c in consts],
        out_specs=tok_spec,
        out_shape=jax.ShapeDtypeStruct((n_tok, D_MODEL), F32),
        scratch_shapes=[pltpu.VMEM((TM, D_FF), BF16)],
        compiler_params=_params(),
        name="ffn2",
    )(h.reshape(n_tok, D_MODEL), p[i].reshape(n_tok, PLE_DIM), *consts)
    return h.reshape(bsz, seqlen, D_MODEL)
```

```python
import functools
import math

import jax
import jax.numpy as jnp
from jax import lax
from jax.experimental import pallas as pl
from jax.experimental.pallas import tpu as pltpu

F32 = jnp.float32
BF16 = jnp.bfloat16

D_MODEL = 1024
D_FF = 2816
PLE_DIM = 256
SSM_WIDTH = 512
POOL_WIDTH = 512
SSM_GROUP = 16
SSM_GROUPS = 32
SSM_STATE = 64
POOL_WINDOWS = (2, 4, 8, 16)
FFN_RES = 0.5
EPS = 1e-6
GELU_C0 = math.sqrt(2.0 / math.pi)
GELU_C1 = 0.044715 * GELU_C0

LANES = 128
SUBLANES = 8
BF16_SUBLANES = 16
V7X_VMEM_BYTES = 64 * 1024 * 1024
FOLD = LANES // SSM_GROUP
PAIRS = SSM_GROUPS // 2
PAIR_W = 2 * LANES
STATE_W = PAIRS * PAIR_W
SSM_VREGS = SSM_WIDTH // LANES

TM = 1024
TT = 128
KROWS = TT // FOLD
PITCH = TT + SUBLANES
FF_CHUNK = 256
TAIL_PARTS = 4
POOL_TAIL = 32
VMEM_LIMIT = V7X_VMEM_BYTES * 7 // 8


def _rms(x, g):
    return x * lax.rsqrt(jnp.mean(x * x, axis=-1, keepdims=True) + EPS) * g


def _dot(a, b):
    return jnp.dot(a, b, preferred_element_type=F32)


def _rstd(x):
    return lax.rsqrt(jnp.mean(x * x, axis=-1, keepdims=True) + EPS)


def _swiglu_hidden(x, w1_ref, w3_ref, act_ref, before_chunk=None):
    xb = x.astype(BF16)
    r = jnp.broadcast_to(_rstd(x), (x.shape[0], FF_CHUNK))
    for c in range(D_FF // FF_CHUNK):
        sl = slice(c * FF_CHUNK, (c + 1) * FF_CHUNK)
        if before_chunk is not None:
            before_chunk(c)
        g = _dot(xb, w1_ref[:, sl]) * r
        u = _dot(xb, w3_ref[:, sl]) * r
        act_ref[:, sl] = (g * jax.nn.sigmoid(g) * u).astype(BF16)


def _swiglu(x, w1_ref, w3_ref, w2_ref, act_ref):
    _swiglu_hidden(x, w1_ref, w3_ref, act_ref)
    return _dot(act_ref[...], w2_ref[...])


def _ffn1_kernel(*refs, side_gain):
    x_ref, w1_hbm, w3_hbm, w2_hbm = refs[:4]
    n_side = len(side_gain)
    n_side_in = n_side + sum(side_gain)
    side_in = iter(refs[4:4 + n_side_in])
    o_ref = refs[4 + n_side_in]
    side_out = refs[5 + n_side_in:5 + n_side_in + n_side]
    w1_ref, w3_ref, w2_ref, act_ref, sem = refs[-5:]
    n_chunks = D_FF // FF_CHUNK

    def chunk_copies(c):
        sl = slice(c * FF_CHUNK, (c + 1) * FF_CHUNK)
        return (pltpu.make_async_copy(w1_hbm.at[:, sl], w1_ref.at[:, sl], sem.at[0, c]),
                pltpu.make_async_copy(w3_hbm.at[:, sl], w3_ref.at[:, sl], sem.at[1, c]))

    w2_copy = pltpu.make_async_copy(w2_hbm, w2_ref, sem.at[2, 0])

    def wait_chunk(c):
        for cp in chunk_copies(c):
            cp.wait()

    first = pl.program_id(0) == 0

    @pl.when(first)
    def _():
        for c in range(n_chunks):
            for cp in chunk_copies(c):
                cp.start()
        w2_copy.start()
        x = x_ref[...]
        _swiglu_hidden(x, w1_ref, w3_ref, act_ref, before_chunk=wait_chunk)
        w2_copy.wait()
        o_ref[...] = x + FFN_RES * _dot(act_ref[...], w2_ref[...])

    @pl.when(jnp.logical_not(first))
    def _():
        x = x_ref[...]
        o_ref[...] = x + FFN_RES * _swiglu(x, w1_ref, w3_ref, w2_ref, act_ref)

    for has_gain, out_ref in zip(side_gain, side_out):
        w = next(side_in)[...]
        if has_gain:
            w = w * next(side_in)[...]
        out_ref[...] = w.astype(BF16)


def _ffn2_kernel(h_ref, p_ref, w1_ref, w3_ref, w2_ref, wg_ref, wp_ref, pnorm_ref, fnorm_ref,
                 o_ref, act_ref):
    emb = _rms(_dot(p_ref[...].astype(BF16), wp_ref[...]), pnorm_ref[...])
    _swiglu_hidden(h_ref[...], w1_ref, w3_ref, act_ref)
    for hh in range(TAIL_PARTS):
        rs = slice(hh * (TM // TAIL_PARTS), (hh + 1) * (TM // TAIL_PARTS))
        hr = h_ref[rs, :] + FFN_RES * _dot(act_ref[rs, :], w2_ref[...])
        gate = jax.nn.sigmoid(_dot(hr.astype(BF16), wg_ref[...]) * _rstd(hr))
        o_ref[rs, :] = _rms(hr + gate * emb[rs, :], fnorm_ref[...])


def _block_transpose(xs, lane_blk):
    xs = list(xs)
    for s in (4, 2, 1):
        upper = (lane_blk & s) != 0
        nxt = list(xs)
        for r in range(len(xs)):
            if r & s:
                continue
            a, b = xs[r], xs[r + s]
            nxt[r] = jnp.where(upper, pltpu.roll(b, SSM_GROUP * s, 1), a)
            nxt[r + s] = jnp.where(upper, b, pltpu.roll(a, LANES - SSM_GROUP * s, 1))
        xs = nxt
    return xs


def _mixer_kernel(h_ref, win_ref, btsup_ref, csup_ref, a8_ref,
                  dskip_ref, wglu_ref, bglu_ref, wpool_ref, wout_ref, o_ref,
                  state_ref, zs_ref, ys_ref, vbuf_ref, s2_ref, s4_ref, s8_ref):
    l = pl.program_id(0)
    nb = h_ref.shape[0]
    nr = TT + POOL_TAIL
    rows = nb * KROWS

    @pl.when(l == 0)
    def _():
        state_ref[...] = jnp.zeros_like(state_ref)
        vbuf_ref[:, 0:POOL_TAIL, :] = jnp.zeros((nb, POOL_TAIL, POOL_WIDTH), F32)

    @pl.when(l > 0)
    def _():
        vbuf_ref[:, 0:POOL_TAIL, :] = vbuf_ref[:, TT:nr, :]

    h = h_ref[...].reshape(nb * TT, D_MODEL)
    z = _dot(h.astype(BF16), win_ref[...]) * _rstd(h)

    for b in range(nb):
        for v in range(SSM_VREGS):
            zs_ref[v, b * PITCH:b * PITCH + TT, :] = z[b * TT:(b + 1) * TT, v * LANES:(v + 1) * LANES]
        vbuf_ref[b, POOL_TAIL:nr, :] = z[b * TT:(b + 1) * TT, SSM_WIDTH:]

    lane_blk = lax.broadcasted_iota(jnp.int32, (rows, LANES), 1) // SSM_GROUP
    for v in range(SSM_VREGS):
        toks = [jnp.concatenate([zs_ref[v, pl.ds(FOLD * k + j, nb, stride=PITCH), :]
                                 for k in range(KROWS)], axis=0) for j in range(FOLD)]
        folded = _block_transpose(toks, lane_blk)
        ysup = []
        for pp in range(FOLD // 2):
            pair = v * (FOLD // 2) + pp
            base = pair * PAIR_W
            usup = jnp.concatenate([folded[2 * pp], folded[2 * pp + 1]], axis=1).astype(BF16)
            xy = _dot(usup, btsup_ref[pair])
            ar = jnp.broadcast_to(a8_ref[0:1, base:base + LANES], (nb, LANES))
            ai = jnp.broadcast_to(a8_ref[0:1, base + LANES:base + PAIR_W], (nb, LANES))
            sr = state_ref[:, base:base + LANES]
            si = state_ref[:, base + LANES:base + PAIR_W]
            prev_r, prev_i = [], []
            for k in range(KROWS):
                prev_r.append(sr)
                prev_i.append(si)
                xr = xy[k * nb:(k + 1) * nb, 0:LANES]
                xi = xy[k * nb:(k + 1) * nb, LANES:PAIR_W]
                sr, si = ar * sr - ai * si + xr, ar * si + ai * sr + xi
            state_ref[:, base:base + LANES] = sr
            state_ref[:, base + LANES:base + PAIR_W] = si
            sprev = jnp.concatenate([jnp.concatenate(prev_r, axis=0),
                                     jnp.concatenate(prev_i, axis=0)], axis=1).astype(BF16)
            y = xy[:, PAIR_W:] + _dot(sprev, csup_ref[pair])
            ysup.append(y[:, :LANES])
            ysup.append(y[:, LANES:])
        outs = _block_transpose(ysup, lane_blk)
        for j in range(FOLD):
            for k in range(KROWS):
                ys_ref[v, pl.ds(FOLD * k + j, nb, stride=PITCH), :] = outs[j][k * nb:(k + 1) * nb, :]

    y = jnp.concatenate(
        [jnp.concatenate([ys_ref[v, b * PITCH:b * PITCH + TT, :] for v in range(SSM_VREGS)], axis=1)
         for b in range(nb)], axis=0)
    y = y + dskip_ref[...] * z[:, :SSM_WIDTH]
    y = y * (0.5 + 0.5 * jnp.tanh(y * (GELU_C0 + GELU_C1 * (y * y))))
    y = y * jax.nn.sigmoid(_dot(y.astype(BF16), wglu_ref[...]) + bglu_ref[...])

    s2_ref[:, 8:nr, :] = vbuf_ref[:, 8:nr, :] + vbuf_ref[:, 7:nr - 1, :]
    s4_ref[:, 16:nr, :] = s2_ref[:, 16:nr, LANES:] + s2_ref[:, 14:nr - 2, LANES:]
    s8_ref[:, 24:nr, :] = s4_ref[:, 24:nr, LANES:] + s4_ref[:, 20:nr - 4, LANES:]
    s16 = s8_ref[:, 32:nr, LANES:] + s8_ref[:, 24:nr - 8, LANES:]
    wins = [s2_ref[:, POOL_TAIL:nr, 0:LANES], s4_ref[:, POOL_TAIL:nr, 0:LANES],
            s8_ref[:, POOL_TAIL:nr, 0:LANES], s16]
    t = l * TT + lax.broadcasted_iota(jnp.int32, (1, TT, 1), 1)
    pooled = []
    for gi, w in enumerate(POOL_WINDOWS):
        inv_count = 1.0 / jnp.minimum(t + 1, w).astype(F32)
        vg = vbuf_ref[:, POOL_TAIL:nr, gi * LANES:(gi + 1) * LANES]
        pg = (wins[gi] * inv_count - vg).reshape(nb * TT, LANES).astype(BF16)
        pooled.append(_dot(pg, wpool_ref[gi]))
    y_pool = jnp.concatenate(pooled, axis=1)

    yn = jnp.concatenate([(y * _rstd(y)).astype(BF16), (y_pool * _rstd(y_pool)).astype(BF16)], axis=1)
    o_ref[...] = (h + _dot(yn, wout_ref[...])).reshape(nb, TT, D_MODEL)


def _cmul(xr, xi, yr, yi):
    return xr * yr - xi * yi, xr * yi + xi * yr


def _dot_nt(a, b):
    return lax.dot_general(a, b, (((1,), (1,)), ((), ())), precision=lax.Precision.HIGHEST,
                           preferred_element_type=F32)


def _tables_kernel(lam_ref, bt_ref, ct_ref, g_ref, w1_ref, w3_ref, w2_ref,
                   btsup_ref, csup_ref, a8_ref, w1o_ref, w3o_ref, w2o_ref, pwr_ref, pwi_ref):
    w1o_ref[...] = (w1_ref[...] * g_ref[...]).astype(BF16)
    w3o_ref[...] = (w3_ref[...] * g_ref[...]).astype(BF16)
    w2o_ref[...] = w2_ref[...].astype(BF16)

    lam_r, lam_i, dt = lam_ref[0:1, :], lam_ref[1:2, :], jnp.exp(lam_ref[2:3, :])
    m = lax.broadcasted_iota(jnp.int32, (2 * FOLD, LANES), 0).astype(F32)
    mag = jnp.exp(m * (lam_r * dt))
    pwr_ref[...] = mag * jnp.cos(m * (lam_i * dt))
    pwi_ref[...] = mag * jnp.sin(m * (lam_i * dt))
    ar, ai = pwr_ref[1:2, :], pwi_ref[1:2, :]
    den = lam_r * lam_r + lam_i * lam_i
    nr = ar - 1.0
    f_r = (nr * lam_r + ai * lam_i) / den
    f_i = (ai * lam_r - nr * lam_i) / den
    bb_r, bb_i = _cmul(f_r, f_i, bt_ref[0], bt_ref[1])
    c_r, c_i = ct_ref[0], ct_ref[1]

    def tiled(x):
        return jnp.concatenate([x] * FOLD, axis=0)

    def power_rows(first, step):
        ks = [first + step * j for j in range(FOLD)]
        return (jnp.concatenate([jnp.broadcast_to(pwr_ref[k:k + 1, :], (SSM_GROUP, LANES)) for k in ks], axis=0),
                jnp.concatenate([jnp.broadcast_to(pwi_ref[k:k + 1, :], (SSM_GROUP, LANES)) for k in ks], axis=0))

    lane = lax.broadcasted_iota(jnp.int32, (LANES, LANES), 1)
    row = lax.broadcasted_iota(jnp.int32, (LANES, LANES), 0)
    g0_lanes = lane < SSM_STATE
    g0_rows = row < SSM_STATE
    zero = jnp.zeros((LANES, LANES), F32)

    s_r, s_i = _cmul(tiled(bb_r), tiled(bb_i), *power_rows(FOLD - 1, -1))
    btsup_ref[0:LANES, 0:LANES] = jnp.where(g0_lanes, s_r, zero).astype(BF16)
    btsup_ref[0:LANES, LANES:PAIR_W] = jnp.where(g0_lanes, s_i, zero).astype(BF16)
    btsup_ref[LANES:PAIR_W, 0:LANES] = jnp.where(g0_lanes, zero, s_r).astype(BF16)
    btsup_ref[LANES:PAIR_W, LANES:PAIR_W] = jnp.where(g0_lanes, zero, s_i).astype(BF16)

    m_r, m_i = _cmul(tiled(c_r), tiled(c_i), *power_rows(1, 1))
    mt_r, mt_i = m_r.T, -m_i.T
    csup_ref[0:LANES, 0:LANES] = jnp.where(g0_rows, mt_r, zero).astype(BF16)
    csup_ref[0:LANES, LANES:PAIR_W] = jnp.where(g0_rows, zero, mt_r).astype(BF16)
    csup_ref[LANES:PAIR_W, 0:LANES] = jnp.where(g0_rows, mt_i, zero).astype(BF16)
    csup_ref[LANES:PAIR_W, LANES:PAIR_W] = jnp.where(g0_rows, zero, mt_i).astype(BF16)

    w_r, w_i = _cmul(tiled(c_r), tiled(c_i), *power_rows(0, 1))
    lane16 = lax.broadcasted_iota(jnp.int32, (SSM_GROUP, LANES), 1)
    for gl in range(2):
        mine = (lane16 < SSM_STATE) if gl == 0 else (lane16 >= SSM_STATE)
        kern = (_dot_nt(jnp.where(mine, bb_r, 0.0), w_r) - _dot_nt(jnp.where(mine, bb_i, 0.0), w_i))
        toe = jnp.concatenate(
            [jnp.where(lane16 >= SSM_GROUP * i, pltpu.roll(kern, SSM_GROUP * i, 1), 0.0) if i else kern
             for i in range(FOLD)], axis=0)
        rows = slice(gl * LANES, (gl + 1) * LANES)
        btsup_ref[rows, PAIR_W + gl * LANES:PAIR_W + (gl + 1) * LANES] = toe.astype(BF16)
        btsup_ref[rows, PAIR_W + (1 - gl) * LANES:PAIR_W + (2 - gl) * LANES] = jnp.zeros((LANES, LANES), BF16)

    a8_ref[...] = jnp.concatenate([pwr_ref[FOLD:FOLD + 1, :], pwi_ref[FOLD:FOLD + 1, :]], axis=1)


def _ssm_tables(a_re, a_im, log_dt, b_re, b_im, c_re, c_im, gains, w1, w3, w2):
    lanes_gn = lambda a: a.astype(F32).reshape(PAIRS, 1, LANES)
    lam = jnp.concatenate(
        [lanes_gn(a_re), lanes_gn(a_im),
         lanes_gn(jnp.broadcast_to(log_dt[:, None], (SSM_GROUPS, SSM_STATE))),
         jnp.zeros((PAIRS, SUBLANES - 3, LANES), F32)], axis=1)
    bt = jnp.stack([b_re, b_im]).astype(F32).reshape(2, PAIRS, 2, SSM_STATE, SSM_GROUP)
    bt = jnp.transpose(bt, (1, 0, 4, 2, 3)).reshape(PAIRS, 2, SSM_GROUP, LANES)
    ct = jnp.stack([c_re, c_im]).astype(F32).reshape(2, PAIRS, 2, SSM_GROUP, SSM_STATE)
    ct = jnp.transpose(ct, (1, 0, 3, 2, 4)).reshape(PAIRS, 2, SSM_GROUP, LANES)
    rows13, rows2 = D_MODEL // PAIRS, D_FF // PAIRS
    step_rows = lambda q: (q, 0)
    return pl.pallas_call(
        _tables_kernel,
        grid=(PAIRS,),
        in_specs=[pl.BlockSpec((None, SUBLANES, LANES), lambda q: (q, 0, 0)),
                  pl.BlockSpec((None, 2, SSM_GROUP, LANES), lambda q: (q, 0, 0, 0)),
                  pl.BlockSpec((None, 2, SSM_GROUP, LANES), lambda q: (q, 0, 0, 0)),
                  pl.BlockSpec((rows13, 1), step_rows), pl.BlockSpec((rows13, D_FF), step_rows),
                  pl.BlockSpec((rows13, D_FF), step_rows), pl.BlockSpec((rows2, D_MODEL), step_rows)],
        out_specs=[pl.BlockSpec((None, PAIR_W, 2 * PAIR_W), lambda q: (q, 0, 0)),
                   pl.BlockSpec((None, PAIR_W, PAIR_W), lambda q: (q, 0, 0)),
                   pl.BlockSpec((1, PAIR_W), lambda q: (0, q)),
                   pl.BlockSpec((rows13, D_FF), step_rows), pl.BlockSpec((rows13, D_FF), step_rows),
                   pl.BlockSpec((rows2, D_MODEL), step_rows)],
        out_shape=[jax.ShapeDtypeStruct((PAIRS, PAIR_W, 2 * PAIR_W), BF16),
                   jax.ShapeDtypeStruct((PAIRS, PAIR_W, PAIR_W), BF16),
                   jax.ShapeDtypeStruct((1, STATE_W), F32),
                   jax.ShapeDtypeStruct((D_MODEL, D_FF), BF16), jax.ShapeDtypeStruct((D_MODEL, D_FF), BF16),
                   jax.ShapeDtypeStruct((D_FF, D_MODEL), BF16)],
        scratch_shapes=[pltpu.VMEM((2 * FOLD, LANES), F32), pltpu.VMEM((2 * FOLD, LANES), F32)],
        compiler_params=pltpu.CompilerParams(dimension_semantics=("arbitrary",)),
        name="s5_tables",
    )(lam, bt, ct, gains, w1.astype(F32), w3.astype(F32), w2.astype(F32))


def _const_spec(shape):
    nd = len(shape)
    return pl.BlockSpec(shape, lambda *_: (0,) * nd)


def _params():
    return pltpu.CompilerParams(dimension_semantics=("arbitrary",), vmem_limit_bytes=VMEM_LIMIT)


def kernel(x, p, ffn1_norm, ffn1_w1, ffn1_w3, ffn1_w2, mix_norm, w_in, a_re, a_im, log_dt, b_re, b_im,
           c_re, c_im, d_skip, w_glu, b_glu, w_pool, pool_scale, ssm_out_norm, pool_out_norm, w_out,
           ffn2_norm, ffn2_w1, ffn2_w3, ffn2_w2, ple_gate_norm, w_ple_gate, w_ple_proj, ple_norm,
           final_norm):
    bsz, seqlen, _ = x.shape
    depth = ffn1_w1.shape[0]
    n_tok = bsz * seqlen
    assert n_tok % TM == 0 and seqlen % TT == 0 and bsz == SUBLANES

    row = lambda a: a.reshape(1, -1).astype(F32)
    tok_spec = pl.BlockSpec((TM, D_MODEL), lambda t: (t, 0))
    seq_spec = pl.BlockSpec((bsz, TT, D_MODEL), lambda l: (0, l, 0))

    assert depth == 1
    i = 0

    gains = jnp.concatenate([ffn1_norm[i], ffn2_norm[i], ple_gate_norm[i], mix_norm[i], ssm_out_norm[i],
                             pool_out_norm[i]]).astype(F32).reshape(-1, 1)
    pool_groups = len(POOL_WINDOWS)
    side = [(ffn2_w1[i], 1), (ffn2_w3[i], 1), (ffn2_w2[i], None), (w_ple_gate[i], 2), (w_ple_proj[i], None),
            (w_in[i], 3), (w_out[i], 4), (w_glu[i], None), (w_pool[i].reshape(pool_groups * LANES, LANES), "POOL")]
    n_steps = n_tok // TM
    side_args, side_in_specs, side_out_specs, side_out_shapes = [], [], [], []
    for w, scale in side:
        rows, cols = w.shape
        units = rows // BF16_SUBLANES
        n_blk = max(d for d in range(1, n_steps + 1) if units % d == 0)
        blk = rows // n_blk
        imap = functools.partial(lambda t, first, last: (first + jnp.minimum(t, last), 0), first=0, last=n_blk - 1)
        side_args.append(w.astype(F32))
        side_in_specs.append(pl.BlockSpec((blk, cols), imap))
        if scale == "POOL":
            per_group = n_blk // pool_groups
            side_args.append(pool_scale[i].astype(F32).reshape(pool_groups, 1, LANES))
            side_in_specs.append(pl.BlockSpec(
                (None, 1, LANES), functools.partial(lambda t, last, per: (jnp.minimum(t, last) // per, 0, 0),
                                                    last=n_blk - 1, per=per_group)))
        elif scale is not None:
            assert rows == D_MODEL
            side_args.append(gains)
            side_in_specs.append(pl.BlockSpec((blk, 1), functools.partial(imap, first=scale * n_blk)))
        side_out_specs.append(pl.BlockSpec((blk, cols), imap))
        side_out_shapes.append(jax.ShapeDtypeStruct((rows, cols), BF16))

    btsup, csup, a8, w1a, w3a, w2a = _ssm_tables(a_re[i], a_im[i], log_dt[i], b_re[i], b_im[i], c_re[i], c_im[i],
                                                 gains, ffn1_w1[i], ffn1_w3[i], ffn1_w2[i])

    h, w1b, w3b, w2b, wgb, wpb, winb, woutb, wglub, wpoolb = pl.pallas_call(
        functools.partial(_ffn1_kernel, side_gain=tuple(scale is not None for _, scale in side)),
        grid=(n_steps,),
        in_specs=[tok_spec] + [pl.BlockSpec(memory_space=pl.ANY)] * 3 + side_in_specs,
        out_specs=[tok_spec] + side_out_specs,
        out_shape=[jax.ShapeDtypeStruct((n_tok, D_MODEL), F32)] + side_out_shapes,
        scratch_shapes=[pltpu.VMEM((D_MODEL, D_FF), BF16), pltpu.VMEM((D_MODEL, D_FF), BF16),
                        pltpu.VMEM((D_FF, D_MODEL), BF16), pltpu.VMEM((TM, D_FF), BF16),
                        pltpu.SemaphoreType.DMA((3, D_FF // FF_CHUNK))],
        compiler_params=_params(),
        name="ffn1",
    )(x.reshape(n_tok, D_MODEL), w1a, w3a, w2a, *side_args)

    consts = [winb, btsup, csup, a8, row(d_skip[i]), wglub, row(b_glu[i]),
              wpoolb.reshape(pool_groups, LANES, LANES), woutb]
    nr = TT + POOL_TAIL
    h = pl.pallas_call(
        _mixer_kernel,
        grid=(seqlen // TT,),
        in_specs=[seq_spec] + [_const_spec(c.shape) for c in consts],
        out_specs=seq_spec,
        out_shape=jax.ShapeDtypeStruct((bsz, seqlen, D_MODEL), F32),
        scratch_shapes=[pltpu.VMEM((bsz, STATE_W), F32),
                        pltpu.VMEM((SSM_VREGS, bsz * PITCH, LANES), F32),
                        pltpu.VMEM((SSM_VREGS, bsz * PITCH, LANES), F32),
                        pltpu.VMEM((bsz, nr, POOL_WIDTH), F32),
                        pltpu.VMEM((bsz, nr, POOL_WIDTH), F32),
                        pltpu.VMEM((bsz, nr, POOL_WIDTH - LANES), F32),
                        pltpu.VMEM((bsz, nr, POOL_WIDTH - 2 * LANES), F32)],
        compiler_params=_params(),
        name="mixer",
    )(h.reshape(bsz, seqlen, D_MODEL), *consts)

    consts = [w1b, w3b, w2b, wgb, wpb, row(ple_norm[i]), row(final_norm)]
    h = pl.pallas_call(
        _ffn2_kernel,
        grid=(n_tok // TM,),
        in_specs=[tok_spec, pl.BlockSpec((TM, PLE_DIM), lambda t: (t, 0))]
                 + [_const_spec(c.shape) for c in consts],
        out_specs=tok_spec,
        out_shape=jax.ShapeDtypeStruct((n_tok, D_MODEL), F32),
        scratch_shapes=[pltpu.VMEM((TM, D_FF), BF16)],
        compiler_params=_params(),
        name="ffn2",
    )(h.reshape(n_tok, D_MODEL), p[i].reshape(n_tok, PLE_DIM), *consts)
    return h.reshape(bsz, seqlen, D_MODEL)
```

```python
import functools
import math

import jax
import jax.numpy as jnp
from jax import lax
from jax.experimental import pallas as pl
from jax.experimental.pallas import tpu as pltpu

F32 = jnp.float32
BF16 = jnp.bfloat16

D_MODEL = 1024
D_FF = 2816
PLE_DIM = 256
SSM_WIDTH = 512
POOL_WIDTH = 512
SSM_GROUP = 16
SSM_GROUPS = 32
SSM_STATE = 64
POOL_WINDOWS = (2, 4, 8, 16)
FFN_RES = 0.5
EPS = 1e-6
GELU_C0 = math.sqrt(2.0 / math.pi)
GELU_C1 = 0.044715 * GELU_C0

LANES = 128
SUBLANES = 8
BF16_SUBLANES = 16
V7X_VMEM_BYTES = 64 * 1024 * 1024
FOLD = LANES // SSM_GROUP
PAIRS = SSM_GROUPS // 2
PAIR_W = 2 * LANES
STATE_W = PAIRS * PAIR_W
SSM_VREGS = SSM_WIDTH // LANES

TM = 1024
TT = 128
KROWS = TT // FOLD
PITCH = TT + SUBLANES
FF_CHUNK = 256
TAIL_PARTS = 4
BACK_PARTS = 2
POOL_TAIL = 32
VMEM_LIMIT = V7X_VMEM_BYTES * 7 // 8


def _rms(x, g):
    return x * lax.rsqrt(jnp.mean(x * x, axis=-1, keepdims=True) + EPS) * g


def _dot(a, b):
    return jnp.dot(a, b, preferred_element_type=F32)


def _rstd(x):
    return lax.rsqrt(jnp.mean(x * x, axis=-1, keepdims=True) + EPS)


def _swiglu_hidden(x, w1_ref, w3_ref, act_ref):
    xb = x.astype(BF16)
    r = jnp.broadcast_to(_rstd(x), (x.shape[0], FF_CHUNK))
    for c in range(D_FF // FF_CHUNK):
        sl = slice(c * FF_CHUNK, (c + 1) * FF_CHUNK)
        g = _dot(xb, w1_ref[:, sl]) * r
        u = _dot(xb, w3_ref[:, sl]) * r
        act_ref[:, sl] = (g * jax.nn.sigmoid(g) * u).astype(BF16)


def _swiglu(x, w1_ref, w3_ref, w2_ref, act_ref):
    _swiglu_hidden(x, w1_ref, w3_ref, act_ref)
    return _dot(act_ref[...], w2_ref[...])


def _ffn1_kernel(*refs, side_gain):
    x_ref, w1_ref, w3_ref, w2_ref = refs[:4]
    n_side = len(side_gain)
    n_side_in = n_side + sum(side_gain)
    side_in = iter(refs[4:4 + n_side_in])
    o_ref = refs[4 + n_side_in]
    side_out = refs[5 + n_side_in:5 + n_side_in + n_side]
    act_ref = refs[-1]

    x = x_ref[...]
    o_ref[...] = x + FFN_RES * _swiglu(x, w1_ref, w3_ref, w2_ref, act_ref)

    for has_gain, out_ref in zip(side_gain, side_out):
        w = next(side_in)[...]
        if has_gain:
            w = w * next(side_in)[...]
        out_ref[...] = w.astype(BF16)


def _ffn2_kernel(h_ref, p_ref, w1_ref, w3_ref, w2_ref, wg_ref, wp_ref, pnorm_ref, fnorm_ref,
                 o_ref, act_ref):
    emb = _rms(_dot(p_ref[...].astype(BF16), wp_ref[...]), pnorm_ref[...])
    _swiglu_hidden(h_ref[...], w1_ref, w3_ref, act_ref)
    for hh in range(TAIL_PARTS):
        rs = slice(hh * (TM // TAIL_PARTS), (hh + 1) * (TM // TAIL_PARTS))
        hr = h_ref[rs, :] + FFN_RES * _dot(act_ref[rs, :], w2_ref[...])
        gate = jax.nn.sigmoid(_dot(hr.astype(BF16), wg_ref[...]) * _rstd(hr))
        o_ref[rs, :] = _rms(hr + gate * emb[rs, :], fnorm_ref[...])


def _block_transpose(xs, lane_blk):
    xs = list(xs)
    for s in (4, 2, 1):
        upper = (lane_blk & s) != 0
        nxt = list(xs)
        for r in range(len(xs)):
            if r & s:
                continue
            a, b = xs[r], xs[r + s]
            nxt[r] = jnp.where(upper, pltpu.roll(b, SSM_GROUP * s, 1), a)
            nxt[r + s] = jnp.where(upper, b, pltpu.roll(a, LANES - SSM_GROUP * s, 1))
        xs = nxt
    return xs


def _mixer_kernel(h_ref, win_ref, btsup_ref, csup_ref, a8_ref,
                  dskip_ref, wglu_ref, bglu_ref, wpool_ref, wout_ref, o_ref,
                  state_ref, zs_ref, ys_ref, vbuf_ref, s2_ref, s4_ref, s8_ref):
    l = pl.program_id(0)
    nb = h_ref.shape[0]
    nr = TT + POOL_TAIL
    rows = nb * KROWS

    @pl.when(l == 0)
    def _():
        state_ref[...] = jnp.zeros_like(state_ref)
        vbuf_ref[:, 0:POOL_TAIL, :] = jnp.zeros((nb, POOL_TAIL, POOL_WIDTH), F32)

    @pl.when(l > 0)
    def _():
        vbuf_ref[:, 0:POOL_TAIL, :] = vbuf_ref[:, TT:nr, :]

    h = h_ref[...].reshape(nb * TT, D_MODEL)
    z = _dot(h.astype(BF16), win_ref[...]) * _rstd(h)

    for b in range(nb):
        for v in range(SSM_VREGS):
            zs_ref[v, b * PITCH:b * PITCH + TT, :] = z[b * TT:(b + 1) * TT, v * LANES:(v + 1) * LANES]
        vbuf_ref[b, POOL_TAIL:nr, :] = z[b * TT:(b + 1) * TT, SSM_WIDTH:]

    lane_blk = lax.broadcasted_iota(jnp.int32, (rows, LANES), 1) // SSM_GROUP
    for v in range(SSM_VREGS):
        toks = [jnp.concatenate([zs_ref[v, pl.ds(FOLD * k + j, nb, stride=PITCH), :]
                                 for k in range(KROWS)], axis=0) for j in range(FOLD)]
        folded = _block_transpose(toks, lane_blk)
        ysup = []
        for pp in range(FOLD // 2):
            pair = v * (FOLD // 2) + pp
            base = pair * PAIR_W
            usup = jnp.concatenate([folded[2 * pp], folded[2 * pp + 1]], axis=1).astype(BF16)
            xy = _dot(usup, btsup_ref[pair])
            ar = jnp.broadcast_to(a8_ref[0:1, base:base + LANES], (nb, LANES))
            ai = jnp.broadcast_to(a8_ref[0:1, base + LANES:base + PAIR_W], (nb, LANES))
            sr = state_ref[:, base:base + LANES]
            si = state_ref[:, base + LANES:base + PAIR_W]
            prev_r, prev_i = [], []
            for k in range(KROWS):
                prev_r.append(sr)
                prev_i.append(si)
                xr = xy[k * nb:(k + 1) * nb, 0:LANES]
                xi = xy[k * nb:(k + 1) * nb, LANES:PAIR_W]
                sr, si = ar * sr - ai * si + xr, ar * si + ai * sr + xi
            state_ref[:, base:base + LANES] = sr
            state_ref[:, base + LANES:base + PAIR_W] = si
            sprev = jnp.concatenate([jnp.concatenate(prev_r, axis=0),
                                     jnp.concatenate(prev_i, axis=0)], axis=1).astype(BF16)
            y = xy[:, PAIR_W:] + _dot(sprev, csup_ref[pair])
            ysup.append(y[:, :LANES])
            ysup.append(y[:, LANES:])
        outs = _block_transpose(ysup, lane_blk)
        for j in range(FOLD):
            for k in range(KROWS):
                ys_ref[v, pl.ds(FOLD * k + j, nb, stride=PITCH), :] = outs[j][k * nb:(k + 1) * nb, :]

    s2_ref[:, 8:nr, :] = vbuf_ref[:, 8:nr, :] + vbuf_ref[:, 7:nr - 1, :]
    s4_ref[:, 16:nr, :] = s2_ref[:, 16:nr, LANES:] + s2_ref[:, 14:nr - 2, LANES:]
    s8_ref[:, 24:nr, :] = s4_ref[:, 24:nr, LANES:] + s4_ref[:, 20:nr - 4, LANES:]
    t = l * TT + lax.broadcasted_iota(jnp.int32, (1, TT, 1), 1)

    nbp = nb // BACK_PARTS
    for part in range(BACK_PARTS):
        bsl = slice(part * nbp, (part + 1) * nbp)
        rs = slice(part * nbp * TT, (part + 1) * nbp * TT)
        y = jnp.concatenate(
            [jnp.concatenate([ys_ref[v, b * PITCH:b * PITCH + TT, :] for v in range(SSM_VREGS)], axis=1)
             for b in range(part * nbp, (part + 1) * nbp)], axis=0)
        y = y + dskip_ref[...] * z[rs, :SSM_WIDTH]
        y = y * (0.5 + 0.5 * jnp.tanh(y * (GELU_C0 + GELU_C1 * (y * y))))
        y = y * jax.nn.sigmoid(_dot(y.astype(BF16), wglu_ref[...]) + bglu_ref[...])

        s16 = s8_ref[bsl, 32:nr, LANES:] + s8_ref[bsl, 24:nr - 8, LANES:]
        wins = [s2_ref[bsl, POOL_TAIL:nr, 0:LANES], s4_ref[bsl, POOL_TAIL:nr, 0:LANES],
                s8_ref[bsl, POOL_TAIL:nr, 0:LANES], s16]
        pooled = []
        for gi, w in enumerate(POOL_WINDOWS):
            inv_count = 1.0 / jnp.minimum(t + 1, w).astype(F32)
            vg = vbuf_ref[bsl, POOL_TAIL:nr, gi * LANES:(gi + 1) * LANES]
            pg = (wins[gi] * inv_count - vg).reshape(nbp * TT, LANES).astype(BF16)
            pooled.append(_dot(pg, wpool_ref[gi]))
        y_pool = jnp.concatenate(pooled, axis=1)

        yn = jnp.concatenate([(y * _rstd(y)).astype(BF16), (y_pool * _rstd(y_pool)).astype(BF16)], axis=1)
        o_ref[bsl, :, :] = (h[rs, :] + _dot(yn, wout_ref[...])).reshape(nbp, TT, D_MODEL)


def _cmul(xr, xi, yr, yi):
    return xr * yr - xi * yi, xr * yi + xi * yr


def _dot_nt(a, b):
    return lax.dot_general(a, b, (((1,), (1,)), ((), ())), precision=lax.Precision.HIGHEST,
                           preferred_element_type=F32)


def _tables_kernel(lam_ref, bt_ref, ct_ref, g_ref, w1_ref, w3_ref, w2_ref,
                   btsup_ref, csup_ref, a8_ref, w1o_ref, w3o_ref, w2o_ref, pwr_ref, pwi_ref):
    w1o_ref[...] = (w1_ref[...] * g_ref[...]).astype(BF16)
    w3o_ref[...] = (w3_ref[...] * g_ref[...]).astype(BF16)
    w2o_ref[...] = w2_ref[...].astype(BF16)

    lam_r, lam_i, dt = lam_ref[0:1, :], lam_ref[1:2, :], jnp.exp(lam_ref[2:3, :])
    m = lax.broadcasted_iota(jnp.int32, (2 * FOLD, LANES), 0).astype(F32)
    mag = jnp.exp(m * (lam_r * dt))
    pwr_ref[...] = mag * jnp.cos(m * (lam_i * dt))
    pwi_ref[...] = mag * jnp.sin(m * (lam_i * dt))
    ar, ai = pwr_ref[1:2, :], pwi_ref[1:2, :]
    den = lam_r * lam_r + lam_i * lam_i
    nr = ar - 1.0
    f_r = (nr * lam_r + ai * lam_i) / den
    f_i = (ai * lam_r - nr * lam_i) / den
    bb_r, bb_i = _cmul(f_r, f_i, bt_ref[0], bt_ref[1])
    c_r, c_i = ct_ref[0], ct_ref[1]

    def tiled(x):
        return jnp.concatenate([x] * FOLD, axis=0)

    def power_rows(first, step):
        ks = [first + step * j for j in range(FOLD)]
        return (jnp.concatenate([jnp.broadcast_to(pwr_ref[k:k + 1, :], (SSM_GROUP, LANES)) for k in ks], axis=0),
                jnp.concatenate([jnp.broadcast_to(pwi_ref[k:k + 1, :], (SSM_GROUP, LANES)) for k in ks], axis=0))

    lane = lax.broadcasted_iota(jnp.int32, (LANES, LANES), 1)
    row = lax.broadcasted_iota(jnp.int32, (LANES, LANES), 0)
    g0_lanes = lane < SSM_STATE
    g0_rows = row < SSM_STATE
    zero = jnp.zeros((LANES, LANES), F32)

    s_r, s_i = _cmul(tiled(bb_r), tiled(bb_i), *power_rows(FOLD - 1, -1))
    btsup_ref[0:LANES, 0:LANES] = jnp.where(g0_lanes, s_r, zero).astype(BF16)
    btsup_ref[0:LANES, LANES:PAIR_W] = jnp.where(g0_lanes, s_i, zero).astype(BF16)
    btsup_ref[LANES:PAIR_W, 0:LANES] = jnp.where(g0_lanes, zero, s_r).astype(BF16)
    btsup_ref[LANES:PAIR_W, LANES:PAIR_W] = jnp.where(g0_lanes, zero, s_i).astype(BF16)

    m_r, m_i = _cmul(tiled(c_r), tiled(c_i), *power_rows(1, 1))
    mt_r, mt_i = m_r.T, -m_i.T
    csup_ref[0:LANES, 0:LANES] = jnp.where(g0_rows, mt_r, zero).astype(BF16)
    csup_ref[0:LANES, LANES:PAIR_W] = jnp.where(g0_rows, zero, mt_r).astype(BF16)
    csup_ref[LANES:PAIR_W, 0:LANES] = jnp.where(g0_rows, mt_i, zero).astype(BF16)
    csup_ref[LANES:PAIR_W, LANES:PAIR_W] = jnp.where(g0_rows, zero, mt_i).astype(BF16)

    w_r, w_i = _cmul(tiled(c_r), tiled(c_i), *power_rows(0, 1))
    lane16 = lax.broadcasted_iota(jnp.int32, (SSM_GROUP, LANES), 1)
    for gl in range(2):
        mine = (lane16 < SSM_STATE) if gl == 0 else (lane16 >= SSM_STATE)
        kern = (_dot_nt(jnp.where(mine, bb_r, 0.0), w_r) - _dot_nt(jnp.where(mine, bb_i, 0.0), w_i))
        toe = jnp.concatenate(
            [jnp.where(lane16 >= SSM_GROUP * i, pltpu.roll(kern, SSM_GROUP * i, 1), 0.0) if i else kern
             for i in range(FOLD)], axis=0)
        rows = slice(gl * LANES, (gl + 1) * LANES)
        btsup_ref[rows, PAIR_W + gl * LANES:PAIR_W + (gl + 1) * LANES] = toe.astype(BF16)
        btsup_ref[rows, PAIR_W + (1 - gl) * LANES:PAIR_W + (2 - gl) * LANES] = jnp.zeros((LANES, LANES), BF16)

    a8_ref[...] = jnp.concatenate([pwr_ref[FOLD:FOLD + 1, :], pwi_ref[FOLD:FOLD + 1, :]], axis=1)


def _ssm_tables(a_re, a_im, log_dt, b_re, b_im, c_re, c_im, gains, w1, w3, w2):
    lanes_gn = lambda a: a.astype(F32).reshape(PAIRS, 1, LANES)
    lam = jnp.concatenate(
        [lanes_gn(a_re), lanes_gn(a_im),
         lanes_gn(jnp.broadcast_to(log_dt[:, None], (SSM_GROUPS, SSM_STATE))),
         jnp.zeros((PAIRS, SUBLANES - 3, LANES), F32)], axis=1)
    bt = jnp.stack([b_re, b_im]).astype(F32).reshape(2, PAIRS, 2, SSM_STATE, SSM_GROUP)
    bt = jnp.transpose(bt, (1, 0, 4, 2, 3)).reshape(PAIRS, 2, SSM_GROUP, LANES)
    ct = jnp.stack([c_re, c_im]).astype(F32).reshape(2, PAIRS, 2, SSM_GROUP, SSM_STATE)
    ct = jnp.transpose(ct, (1, 0, 3, 2, 4)).reshape(PAIRS, 2, SSM_GROUP, LANES)
    rows13, rows2 = D_MODEL // PAIRS, D_FF // PAIRS
    step_rows = lambda q: (q, 0)
    return pl.pallas_call(
        _tables_kernel,
        grid=(PAIRS,),
        in_specs=[pl.BlockSpec((None, SUBLANES, LANES), lambda q: (q, 0, 0)),
                  pl.BlockSpec((None, 2, SSM_GROUP, LANES), lambda q: (q, 0, 0, 0)),
                  pl.BlockSpec((None, 2, SSM_GROUP, LANES), lambda q: (q, 0, 0, 0)),
                  pl.BlockSpec((rows13, 1), step_rows), pl.BlockSpec((rows13, D_FF), step_rows),
                  pl.BlockSpec((rows13, D_FF), step_rows), pl.BlockSpec((rows2, D_MODEL), step_rows)],
        out_specs=[pl.BlockSpec((None, PAIR_W, 2 * PAIR_W), lambda q: (q, 0, 0)),
                   pl.BlockSpec((None, PAIR_W, PAIR_W), lambda q: (q, 0, 0)),
                   pl.BlockSpec((1, PAIR_W), lambda q: (0, q)),
                   pl.BlockSpec((rows13, D_FF), step_rows), pl.BlockSpec((rows13, D_FF), step_rows),
                   pl.BlockSpec((rows2, D_MODEL), step_rows)],
        out_shape=[jax.ShapeDtypeStruct((PAIRS, PAIR_W, 2 * PAIR_W), BF16),
                   jax.ShapeDtypeStruct((PAIRS, PAIR_W, PAIR_W), BF16),
                   jax.ShapeDtypeStruct((1, STATE_W), F32),
                   jax.ShapeDtypeStruct((D_MODEL, D_FF), BF16), jax.ShapeDtypeStruct((D_MODEL, D_FF), BF16),
                   jax.ShapeDtypeStruct((D_FF, D_MODEL), BF16)],
        scratch_shapes=[pltpu.VMEM((2 * FOLD, LANES), F32), pltpu.VMEM((2 * FOLD, LANES), F32)],
        compiler_params=pltpu.CompilerParams(dimension_semantics=("arbitrary",)),
        name="s5_tables",
    )(lam, bt, ct, gains, w1.astype(F32), w3.astype(F32), w2.astype(F32))


def _const_spec(shape):
    nd = len(shape)
    return pl.BlockSpec(shape, lambda *_: (0,) * nd)


def _params():
    return pltpu.CompilerParams(dimension_semantics=("arbitrary",), vmem_limit_bytes=VMEM_LIMIT)


def kernel(x, p, ffn1_norm, ffn1_w1, ffn1_w3, ffn1_w2, mix_norm, w_in, a_re, a_im, log_dt, b_re, b_im,
           c_re, c_im, d_skip, w_glu, b_glu, w_pool, pool_scale, ssm_out_norm, pool_out_norm, w_out,
           ffn2_norm, ffn2_w1, ffn2_w3, ffn2_w2, ple_gate_norm, w_ple_gate, w_ple_proj, ple_norm,
           final_norm):
    bsz, seqlen, _ = x.shape
    depth = ffn1_w1.shape[0]
    n_tok = bsz * seqlen
    assert n_tok % TM == 0 and seqlen % TT == 0 and bsz == SUBLANES

    row = lambda a: a.reshape(1, -1).astype(F32)
    tok_spec = pl.BlockSpec((TM, D_MODEL), lambda t: (t, 0))
    seq_spec = pl.BlockSpec((bsz, TT, D_MODEL), lambda l: (0, l, 0))

    assert depth == 1
    i = 0

    gains = jnp.concatenate([ffn1_norm[i], ffn2_norm[i], ple_gate_norm[i], mix_norm[i], ssm_out_norm[i],
                             pool_out_norm[i]]).astype(F32).reshape(-1, 1)
    pool_groups = len(POOL_WINDOWS)
    side = [(ffn2_w1[i], 1), (ffn2_w3[i], 1), (ffn2_w2[i], None), (w_ple_gate[i], 2), (w_ple_proj[i], None),
            (w_in[i], 3), (w_out[i], 4), (w_glu[i], None), (w_pool[i].reshape(pool_groups * LANES, LANES), "POOL")]
    n_steps = n_tok // TM
    side_args, side_in_specs, side_out_specs, side_out_shapes = [], [], [], []
    for w, scale in side:
        rows, cols = w.shape
        units = rows // BF16_SUBLANES
        n_blk = max(d for d in range(1, n_steps + 1) if units % d == 0)
        blk = rows // n_blk
        imap = functools.partial(lambda t, first, last: (first + jnp.minimum(t, last), 0), first=0, last=n_blk - 1)
        side_args.append(w.astype(F32))
        side_in_specs.append(pl.BlockSpec((blk, cols), imap))
        if scale == "POOL":
            per_group = n_blk // pool_groups
            side_args.append(pool_scale[i].astype(F32).reshape(pool_groups, 1, LANES))
            side_in_specs.append(pl.BlockSpec(
                (None, 1, LANES), functools.partial(lambda t, last, per: (jnp.minimum(t, last) // per, 0, 0),
                                                    last=n_blk - 1, per=per_group)))
        elif scale is not None:
            assert rows == D_MODEL
            side_args.append(gains)
            side_in_specs.append(pl.BlockSpec((blk, 1), functools.partial(imap, first=scale * n_blk)))
        side_out_specs.append(pl.BlockSpec((blk, cols), imap))
        side_out_shapes.append(jax.ShapeDtypeStruct((rows, cols), BF16))

    btsup, csup, a8, w1a, w3a, w2a = _ssm_tables(a_re[i], a_im[i], log_dt[i], b_re[i], b_im[i], c_re[i], c_im[i],
                                                 gains, ffn1_w1[i], ffn1_w3[i], ffn1_w2[i])

    h, w1b, w3b, w2b, wgb, wpb, winb, woutb, wglub, wpoolb = pl.pallas_call(
        functools.partial(_ffn1_kernel, side_gain=tuple(scale is not None for _, scale in side)),
        grid=(n_steps,),
        in_specs=[tok_spec, _const_spec((D_MODEL, D_FF)), _const_spec((D_MODEL, D_FF)),
                  _const_spec((D_FF, D_MODEL))] + side_in_specs,
        out_specs=[tok_spec] + side_out_specs,
        out_shape=[jax.ShapeDtypeStruct((n_tok, D_MODEL), F32)] + side_out_shapes,
        scratch_shapes=[pltpu.VMEM((TM, D_FF), BF16)],
        compiler_params=_params(),
        name="ffn1",
    )(x.reshape(n_tok, D_MODEL), w1a, w3a, w2a, *side_args)

    consts = [winb, btsup, csup, a8, row(d_skip[i]), wglub, row(b_glu[i]),
              wpoolb.reshape(pool_groups, LANES, LANES), woutb]
    nr = TT + POOL_TAIL
    h = pl.pallas_call(
        _mixer_kernel,
        grid=(seqlen // TT,),
        in_specs=[seq_spec] + [_const_spec(c.shape) for c in consts],
        out_specs=seq_spec,
        out_shape=jax.ShapeDtypeStruct((bsz, seqlen, D_MODEL), F32),
        scratch_shapes=[pltpu.VMEM((bsz, STATE_W), F32),
                        pltpu.VMEM((SSM_VREGS, bsz * PITCH, LANES), F32),
                        pltpu.VMEM((SSM_VREGS, bsz * PITCH, LANES), F32),
                        pltpu.VMEM((bsz, nr, POOL_WIDTH), F32),
                        pltpu.VMEM((bsz, nr, POOL_WIDTH), F32),
                        pltpu.VMEM((bsz, nr, POOL_WIDTH - LANES), F32),
                        pltpu.VMEM((bsz, nr, POOL_WIDTH - 2 * LANES), F32)],
        compiler_params=_params(),
        name="mixer",
    )(h.reshape(bsz, seqlen, D_MODEL), *consts)

    consts = [w1b, w3b, w2b, wgb, wpb, row(ple_norm[i]), row(final_norm)]
    h = pl.pallas_call(
        _ffn2_kernel,
        grid=(n_tok // TM,),
        in_specs=[tok_spec, pl.BlockSpec((TM, PLE_DIM), lambda t: (t, 0))]
                 + [_const_spec(c.shape) for c in consts],
        out_specs=tok_spec,
        out_shape=jax.ShapeDtypeStruct((n_tok, D_MODEL), F32),
        scratch_shapes=[pltpu.VMEM((TM, D_FF), BF16)],
        compiler_params=_params(),
        name="ffn2",
    )(h.reshape(n_tok, D_MODEL), p[i].reshape(n_tok, PLE_DIM), *consts)
    return h.reshape(bsz, seqlen, D_MODEL)
```

```python
import functools
import math

import jax
import jax.numpy as jnp
from jax import lax
from jax.experimental import pallas as pl
from jax.experimental.pallas import tpu as pltpu

F32 = jnp.float32
BF16 = jnp.bfloat16

D_MODEL = 1024
D_FF = 2816
PLE_DIM = 256
SSM_WIDTH = 512
POOL_WIDTH = 512
SSM_GROUP = 16
SSM_GROUPS = 32
SSM_STATE = 64
POOL_WINDOWS = (2, 4, 8, 16)
FFN_RES = 0.5
EPS = 1e-6
GELU_C0 = math.sqrt(2.0 / math.pi)
GELU_C1 = 0.044715 * GELU_C0

LANES = 128
SUBLANES = 8
BF16_SUBLANES = 16
V7X_VMEM_BYTES = 64 * 1024 * 1024
FOLD = LANES // SSM_GROUP
PAIRS = SSM_GROUPS // 2
PAIR_W = 2 * LANES
STATE_W = PAIRS * PAIR_W
SSM_VREGS = SSM_WIDTH // LANES

TM = 1024
TT = 128
KROWS = TT // FOLD
PITCH = TT + SUBLANES
FF_CHUNK = 256
TAIL_PARTS = 4
POOL_TAIL = 32
VMEM_LIMIT = V7X_VMEM_BYTES * 7 // 8


def _rms(x, g):
    return x * lax.rsqrt(jnp.mean(x * x, axis=-1, keepdims=True) + EPS) * g


def _dot(a, b):
    return jnp.dot(a, b, preferred_element_type=F32)


def _rstd(x):
    return lax.rsqrt(jnp.mean(x * x, axis=-1, keepdims=True) + EPS)


def _swiglu_hidden(x, w1_ref, w3_ref, act_ref):
    xb = x.astype(BF16)
    r = jnp.broadcast_to(_rstd(x), (x.shape[0], FF_CHUNK))
    for c in range(D_FF // FF_CHUNK):
        sl = slice(c * FF_CHUNK, (c + 1) * FF_CHUNK)
        g = _dot(xb, w1_ref[:, sl]) * r
        u = _dot(xb, w3_ref[:, sl]) * r
        act_ref[:, sl] = (g * jax.nn.sigmoid(g) * u).astype(BF16)


def _swiglu(x, w1_ref, w3_ref, w2_ref, act_ref):
    _swiglu_hidden(x, w1_ref, w3_ref, act_ref)
    return _dot(act_ref[...], w2_ref[...])


def _ffn1_kernel(*refs, side_gain):
    x_ref, w1_ref, w3_ref, w2_ref = refs[:4]
    n_side = len(side_gain)
    n_side_in = n_side + sum(side_gain)
    side_in = iter(refs[4:4 + n_side_in])
    o_ref = refs[4 + n_side_in]
    side_out = refs[5 + n_side_in:5 + n_side_in + n_side]
    act_ref = refs[-1]

    x = x_ref[...]
    o_ref[...] = x + FFN_RES * _swiglu(x, w1_ref, w3_ref, w2_ref, act_ref)

    for has_gain, out_ref in zip(side_gain, side_out):
        w = next(side_in)[...]
        if has_gain:
            w = w * next(side_in)[...]
        out_ref[...] = w.astype(BF16)


def _ffn2_kernel(h_ref, p_ref, w1_ref, w3_ref, w2_ref, wg_ref, wp_ref, pnorm_ref, fnorm_ref,
                 o_ref, act_ref):
    emb = _rms(_dot(p_ref[...].astype(BF16), wp_ref[...]), pnorm_ref[...])
    _swiglu_hidden(h_ref[...], w1_ref, w3_ref, act_ref)
    for hh in range(TAIL_PARTS):
        rs = slice(hh * (TM // TAIL_PARTS), (hh + 1) * (TM // TAIL_PARTS))
        hr = h_ref[rs, :] + FFN_RES * _dot(act_ref[rs, :], w2_ref[...])
        gate = jax.nn.sigmoid(_dot(hr.astype(BF16), wg_ref[...]) * _rstd(hr))
        o_ref[rs, :] = _rms(hr + gate * emb[rs, :], fnorm_ref[...])


def _block_transpose(xs, lane_blk):
    xs = list(xs)
    for s in (4, 2, 1):
        upper = (lane_blk & s) != 0
        nxt = list(xs)
        for r in range(len(xs)):
            if r & s:
                continue
            a, b = xs[r], xs[r + s]
            nxt[r] = jnp.where(upper, pltpu.roll(b, SSM_GROUP * s, 1), a)
            nxt[r + s] = jnp.where(upper, b, pltpu.roll(a, LANES - SSM_GROUP * s, 1))
        xs = nxt
    return xs


def _mixer_kernel(h_ref, win_ref, btsup_ref, csup_ref, a8_ref,
                  dskip_ref, wglu_ref, bglu_ref, wpool_ref, wout_ref, o_ref,
                  state_ref, zs_ref, ys_ref, vbuf_ref, s2_ref, s4_ref, s8_ref):
    l = pl.program_id(0)
    nb = h_ref.shape[0]
    nr = TT + POOL_TAIL
    rows = nb * KROWS

    @pl.when(l == 0)
    def _():
        state_ref[...] = jnp.zeros_like(state_ref)
        vbuf_ref[:, 0:POOL_TAIL, :] = jnp.zeros((nb, POOL_TAIL, POOL_WIDTH), F32)

    @pl.when(l > 0)
    def _():
        vbuf_ref[:, 0:POOL_TAIL, :] = vbuf_ref[:, TT:nr, :]

    h = h_ref[...].reshape(nb * TT, D_MODEL)
    z = _dot(h.astype(BF16), win_ref[...]) * _rstd(h)

    for b in range(nb):
        for v in range(SSM_VREGS):
            zs_ref[v, b * PITCH:b * PITCH + TT, :] = z[b * TT:(b + 1) * TT, v * LANES:(v + 1) * LANES]
        vbuf_ref[b, POOL_TAIL:nr, :] = z[b * TT:(b + 1) * TT, SSM_WIDTH:]

    lane_blk = lax.broadcasted_iota(jnp.int32, (rows, LANES), 1) // SSM_GROUP
    for v in range(SSM_VREGS):
        toks = [jnp.concatenate([zs_ref[v, pl.ds(FOLD * k + j, nb, stride=PITCH), :]
                                 for k in range(KROWS)], axis=0) for j in range(FOLD)]
        folded = _block_transpose(toks, lane_blk)
        ysup = []
        for pp in range(FOLD // 2):
            pair = v * (FOLD // 2) + pp
            base = pair * PAIR_W
            usup = jnp.concatenate([folded[2 * pp], folded[2 * pp + 1]], axis=1).astype(BF16)
            xy = _dot(usup, btsup_ref[pair])
            ar = jnp.broadcast_to(a8_ref[0:1, base:base + LANES], (nb, LANES))
            ai = jnp.broadcast_to(a8_ref[0:1, base + LANES:base + PAIR_W], (nb, LANES))
            sr = state_ref[:, base:base + LANES]
            si = state_ref[:, base + LANES:base + PAIR_W]
            prev_r, prev_i = [], []
            for k in range(KROWS):
                prev_r.append(sr)
                prev_i.append(si)
                xr = xy[k * nb:(k + 1) * nb, 0:LANES]
                xi = xy[k * nb:(k + 1) * nb, LANES:PAIR_W]
                sr, si = ar * sr - ai * si + xr, ar * si + ai * sr + xi
            state_ref[:, base:base + LANES] = sr
            state_ref[:, base + LANES:base + PAIR_W] = si
            sprev = jnp.concatenate([jnp.concatenate(prev_r, axis=0),
                                     jnp.concatenate(prev_i, axis=0)], axis=1).astype(BF16)
            y = xy[:, PAIR_W:] + _dot(sprev, csup_ref[pair])
            ysup.append(y[:, :LANES])
            ysup.append(y[:, LANES:])
        outs = _block_transpose(ysup, lane_blk)
        for j in range(FOLD):
            for k in range(KROWS):
                ys_ref[v, pl.ds(FOLD * k + j, nb, stride=PITCH), :] = outs[j][k * nb:(k + 1) * nb, :]

    y = jnp.concatenate(
        [jnp.concatenate([ys_ref[v, b * PITCH:b * PITCH + TT, :] for v in range(SSM_VREGS)], axis=1)
         for b in range(nb)], axis=0)
    y = y + dskip_ref[...] * z[:, :SSM_WIDTH]
    y = y * (0.5 + 0.5 * jnp.tanh(y * (GELU_C0 + GELU_C1 * (y * y))))
    y = y * (0.5 + 0.5 * jnp.tanh(0.5 * (_dot(y.astype(BF16), wglu_ref[...]) + bglu_ref[...])))

    s2_ref[:, 8:nr, :] = vbuf_ref[:, 8:nr, :] + vbuf_ref[:, 7:nr - 1, :]
    s4_ref[:, 16:nr, :] = s2_ref[:, 16:nr, LANES:] + s2_ref[:, 14:nr - 2, LANES:]
    s8_ref[:, 24:nr, :] = s4_ref[:, 24:nr, LANES:] + s4_ref[:, 20:nr - 4, LANES:]
    s16 = s8_ref[:, 32:nr, LANES:] + s8_ref[:, 24:nr - 8, LANES:]
    wins = [s2_ref[:, POOL_TAIL:nr, 0:LANES], s4_ref[:, POOL_TAIL:nr, 0:LANES],
            s8_ref[:, POOL_TAIL:nr, 0:LANES], s16]
    t = l * TT + lax.broadcasted_iota(jnp.int32, (1, TT, 1), 1)
    pooled = []
    for gi, w in enumerate(POOL_WINDOWS):
        inv_count = 1.0 / jnp.minimum(t + 1, w).astype(F32)
        vg = vbuf_ref[:, POOL_TAIL:nr, gi * LANES:(gi + 1) * LANES]
        pg = (wins[gi] * inv_count - vg).reshape(nb * TT, LANES).astype(BF16)
        pooled.append(_dot(pg, wpool_ref[gi]))
    y_pool = jnp.concatenate(pooled, axis=1)

    yn = jnp.concatenate([(y * _rstd(y)).astype(BF16), (y_pool * _rstd(y_pool)).astype(BF16)], axis=1)
    o_ref[...] = (h + _dot(yn, wout_ref[...])).reshape(nb, TT, D_MODEL)


def _cmul(xr, xi, yr, yi):
    return xr * yr - xi * yi, xr * yi + xi * yr


def _dot_nt(a, b):
    return lax.dot_general(a, b, (((1,), (1,)), ((), ())), precision=lax.Precision.HIGHEST,
                           preferred_element_type=F32)


def _tables_kernel(lam_ref, bt_ref, ct_ref, g_ref, w1_ref, w3_ref, w2_ref,
                   btsup_ref, csup_ref, a8_ref, w1o_ref, w3o_ref, w2o_ref, pwr_ref, pwi_ref):
    w1o_ref[...] = (w1_ref[...] * g_ref[...]).astype(BF16)
    w3o_ref[...] = (w3_ref[...] * g_ref[...]).astype(BF16)
    w2o_ref[...] = w2_ref[...].astype(BF16)

    lam_r, lam_i, dt = lam_ref[0:1, :], lam_ref[1:2, :], jnp.exp(lam_ref[2:3, :])
    m = lax.broadcasted_iota(jnp.int32, (2 * FOLD, LANES), 0).astype(F32)
    mag = jnp.exp(m * (lam_r * dt))
    pwr_ref[...] = mag * jnp.cos(m * (lam_i * dt))
    pwi_ref[...] = mag * jnp.sin(m * (lam_i * dt))
    ar, ai = pwr_ref[1:2, :], pwi_ref[1:2, :]
    den = lam_r * lam_r + lam_i * lam_i
    nr = ar - 1.0
    f_r = (nr * lam_r + ai * lam_i) / den
    f_i = (ai * lam_r - nr * lam_i) / den
    bb_r, bb_i = _cmul(f_r, f_i, bt_ref[0], bt_ref[1])
    c_r, c_i = ct_ref[0], ct_ref[1]

    def tiled(x):
        return jnp.concatenate([x] * FOLD, axis=0)

    def power_rows(first, step):
        ks = [first + step * j for j in range(FOLD)]
        return (jnp.concatenate([jnp.broadcast_to(pwr_ref[k:k + 1, :], (SSM_GROUP, LANES)) for k in ks], axis=0),
                jnp.concatenate([jnp.broadcast_to(pwi_ref[k:k + 1, :], (SSM_GROUP, LANES)) for k in ks], axis=0))

    lane = lax.broadcasted_iota(jnp.int32, (LANES, LANES), 1)
    row = lax.broadcasted_iota(jnp.int32, (LANES, LANES), 0)
    g0_lanes = lane < SSM_STATE
    g0_rows = row < SSM_STATE
    zero = jnp.zeros((LANES, LANES), F32)

    s_r, s_i = _cmul(tiled(bb_r), tiled(bb_i), *power_rows(FOLD - 1, -1))
    btsup_ref[0:LANES, 0:LANES] = jnp.where(g0_lanes, s_r, zero).astype(BF16)
    btsup_ref[0:LANES, LANES:PAIR_W] = jnp.where(g0_lanes, s_i, zero).astype(BF16)
    btsup_ref[LANES:PAIR_W, 0:LANES] = jnp.where(g0_lanes, zero, s_r).astype(BF16)
    btsup_ref[LANES:PAIR_W, LANES:PAIR_W] = jnp.where(g0_lanes, zero, s_i).astype(BF16)

    m_r, m_i = _cmul(tiled(c_r), tiled(c_i), *power_rows(1, 1))
    mt_r, mt_i = m_r.T, -m_i.T
    csup_ref[0:LANES, 0:LANES] = jnp.where(g0_rows, mt_r, zero).astype(BF16)
    csup_ref[0:LANES, LANES:PAIR_W] = jnp.where(g0_rows, zero, mt_r).astype(BF16)
    csup_ref[LANES:PAIR_W, 0:LANES] = jnp.where(g0_rows, mt_i, zero).astype(BF16)
    csup_ref[LANES:PAIR_W, LANES:PAIR_W] = jnp.where(g0_rows, zero, mt_i).astype(BF16)

    w_r, w_i = _cmul(tiled(c_r), tiled(c_i), *power_rows(0, 1))
    lane16 = lax.broadcasted_iota(jnp.int32, (SSM_GROUP, LANES), 1)
    for gl in range(2):
        mine = (lane16 < SSM_STATE) if gl == 0 else (lane16 >= SSM_STATE)
        kern = (_dot_nt(jnp.where(mine, bb_r, 0.0), w_r) - _dot_nt(jnp.where(mine, bb_i, 0.0), w_i))
        toe = jnp.concatenate(
            [jnp.where(lane16 >= SSM_GROUP * i, pltpu.roll(kern, SSM_GROUP * i, 1), 0.0) if i else kern
             for i in range(FOLD)], axis=0)
        rows = slice(gl * LANES, (gl + 1) * LANES)
        btsup_ref[rows, PAIR_W + gl * LANES:PAIR_W + (gl + 1) * LANES] = toe.astype(BF16)
        btsup_ref[rows, PAIR_W + (1 - gl) * LANES:PAIR_W + (2 - gl) * LANES] = jnp.zeros((LANES, LANES), BF16)

    a8_ref[...] = jnp.concatenate([pwr_ref[FOLD:FOLD + 1, :], pwi_ref[FOLD:FOLD + 1, :]], axis=1)


def _ssm_tables(a_re, a_im, log_dt, b_re, b_im, c_re, c_im, gains, w1, w3, w2):
    lanes_gn = lambda a: a.astype(F32).reshape(PAIRS, 1, LANES)
    lam = jnp.concatenate(
        [lanes_gn(a_re), lanes_gn(a_im),
         lanes_gn(jnp.broadcast_to(log_dt[:, None], (SSM_GROUPS, SSM_STATE))),
         jnp.zeros((PAIRS, SUBLANES - 3, LANES), F32)], axis=1)
    bt = jnp.stack([b_re, b_im]).astype(F32).reshape(2, PAIRS, 2, SSM_STATE, SSM_GROUP)
    bt = jnp.transpose(bt, (1, 0, 4, 2, 3)).reshape(PAIRS, 2, SSM_GROUP, LANES)
    ct = jnp.stack([c_re, c_im]).astype(F32).reshape(2, PAIRS, 2, SSM_GROUP, SSM_STATE)
    ct = jnp.transpose(ct, (1, 0, 3, 2, 4)).reshape(PAIRS, 2, SSM_GROUP, LANES)
    rows13, rows2 = D_MODEL // PAIRS, D_FF // PAIRS
    step_rows = lambda q: (q, 0)
    return pl.pallas_call(
        _tables_kernel,
        grid=(PAIRS,),
        in_specs=[pl.BlockSpec((None, SUBLANES, LANES), lambda q: (q, 0, 0)),
                  pl.BlockSpec((None, 2, SSM_GROUP, LANES), lambda q: (q, 0, 0, 0)),
                  pl.BlockSpec((None, 2, SSM_GROUP, LANES), lambda q: (q, 0, 0, 0)),
                  pl.BlockSpec((rows13, 1), step_rows), pl.BlockSpec((rows13, D_FF), step_rows),
                  pl.BlockSpec((rows13, D_FF), step_rows), pl.BlockSpec((rows2, D_MODEL), step_rows)],
        out_specs=[pl.BlockSpec((None, PAIR_W, 2 * PAIR_W), lambda q: (q, 0, 0)),
                   pl.BlockSpec((None, PAIR_W, PAIR_W), lambda q: (q, 0, 0)),
                   pl.BlockSpec((1, PAIR_W), lambda q: (0, q)),
                   pl.BlockSpec((rows13, D_FF), step_rows), pl.BlockSpec((rows13, D_FF), step_rows),
                   pl.BlockSpec((rows2, D_MODEL), step_rows)],
        out_shape=[jax.ShapeDtypeStruct((PAIRS, PAIR_W, 2 * PAIR_W), BF16),
                   jax.ShapeDtypeStruct((PAIRS, PAIR_W, PAIR_W), BF16),
                   jax.ShapeDtypeStruct((1, STATE_W), F32),
                   jax.ShapeDtypeStruct((D_MODEL, D_FF), BF16), jax.ShapeDtypeStruct((D_MODEL, D_FF), BF16),
                   jax.ShapeDtypeStruct((D_FF, D_MODEL), BF16)],
        scratch_shapes=[pltpu.VMEM((2 * FOLD, LANES), F32), pltpu.VMEM((2 * FOLD, LANES), F32)],
        compiler_params=pltpu.CompilerParams(dimension_semantics=("arbitrary",)),
        name="s5_tables",
    )(lam, bt, ct, gains, w1.astype(F32), w3.astype(F32), w2.astype(F32))


def _const_spec(shape):
    nd = len(shape)
    return pl.BlockSpec(shape, lambda *_: (0,) * nd)


def _params():
    return pltpu.CompilerParams(dimension_semantics=("arbitrary",), vmem_limit_bytes=VMEM_LIMIT)


def kernel(x, p, ffn1_norm, ffn1_w1, ffn1_w3, ffn1_w2, mix_norm, w_in, a_re, a_im, log_dt, b_re, b_im,
           c_re, c_im, d_skip, w_glu, b_glu, w_pool, pool_scale, ssm_out_norm, pool_out_norm, w_out,
           ffn2_norm, ffn2_w1, ffn2_w3, ffn2_w2, ple_gate_norm, w_ple_gate, w_ple_proj, ple_norm,
           final_norm):
    bsz, seqlen, _ = x.shape
    depth = ffn1_w1.shape[0]
    n_tok = bsz * seqlen
    assert n_tok % TM == 0 and seqlen % TT == 0 and bsz == SUBLANES

    row = lambda a: a.reshape(1, -1).astype(F32)
    tok_spec = pl.BlockSpec((TM, D_MODEL), lambda t: (t, 0))
    seq_spec = pl.BlockSpec((bsz, TT, D_MODEL), lambda l: (0, l, 0))

    assert depth == 1
    i = 0

    gains = jnp.concatenate([ffn1_norm[i], ffn2_norm[i], ple_gate_norm[i], mix_norm[i], ssm_out_norm[i],
                             pool_out_norm[i]]).astype(F32).reshape(-1, 1)
    pool_groups = len(POOL_WINDOWS)
    side = [(ffn2_w1[i], 1), (ffn2_w3[i], 1), (ffn2_w2[i], None), (w_ple_gate[i], 2), (w_ple_proj[i], None),
            (w_in[i], 3), (w_out[i], 4), (w_glu[i], None), (w_pool[i].reshape(pool_groups * LANES, LANES), "POOL")]
    n_steps = n_tok // TM
    side_args, side_in_specs, side_out_specs, side_out_shapes = [], [], [], []
    for w, scale in side:
        rows, cols = w.shape
        units = rows // BF16_SUBLANES
        n_blk = max(d for d in range(1, n_steps + 1) if units % d == 0)
        blk = rows // n_blk
        imap = functools.partial(lambda t, first, last: (first + jnp.minimum(t, last), 0), first=0, last=n_blk - 1)
        side_args.append(w.astype(F32))
        side_in_specs.append(pl.BlockSpec((blk, cols), imap))
        if scale == "POOL":
            per_group = n_blk // pool_groups
            side_args.append(pool_scale[i].astype(F32).reshape(pool_groups, 1, LANES))
            side_in_specs.append(pl.BlockSpec(
                (None, 1, LANES), functools.partial(lambda t, last, per: (jnp.minimum(t, last) // per, 0, 0),
                                                    last=n_blk - 1, per=per_group)))
        elif scale is not None:
            assert rows == D_MODEL
            side_args.append(gains)
            side_in_specs.append(pl.BlockSpec((blk, 1), functools.partial(imap, first=scale * n_blk)))
        side_out_specs.append(pl.BlockSpec((blk, cols), imap))
        side_out_shapes.append(jax.ShapeDtypeStruct((rows, cols), BF16))

    btsup, csup, a8, w1a, w3a, w2a = _ssm_tables(a_re[i], a_im[i], log_dt[i], b_re[i], b_im[i], c_re[i], c_im[i],
                                                 gains, ffn1_w1[i], ffn1_w3[i], ffn1_w2[i])

    h, w1b, w3b, w2b, wgb, wpb, winb, woutb, wglub, wpoolb = pl.pallas_call(
        functools.partial(_ffn1_kernel, side_gain=tuple(scale is not None for _, scale in side)),
        grid=(n_steps,),
        in_specs=[tok_spec, _const_spec((D_MODEL, D_FF)), _const_spec((D_MODEL, D_FF)),
                  _const_spec((D_FF, D_MODEL))] + side_in_specs,
        out_specs=[tok_spec] + side_out_specs,
        out_shape=[jax.ShapeDtypeStruct((n_tok, D_MODEL), F32)] + side_out_shapes,
        scratch_shapes=[pltpu.VMEM((TM, D_FF), BF16)],
        compiler_params=_params(),
        name="ffn1",
    )(x.reshape(n_tok, D_MODEL), w1a, w3a, w2a, *side_args)

    consts = [winb, btsup, csup, a8, row(d_skip[i]), wglub, row(b_glu[i]),
              wpoolb.reshape(pool_groups, LANES, LANES), woutb]
    nr = TT + POOL_TAIL
    h = pl.pallas_call(
        _mixer_kernel,
        grid=(seqlen // TT,),
        in_specs=[seq_spec] + [_const_spec(c.shape) for c in consts],
        out_specs=seq_spec,
        out_shape=jax.ShapeDtypeStruct((bsz, seqlen, D_MODEL), F32),
        scratch_shapes=[pltpu.VMEM((bsz, STATE_W), F32),
                        pltpu.VMEM((SSM_VREGS, bsz * PITCH, LANES), F32),
                        pltpu.VMEM((SSM_VREGS, bsz * PITCH, LANES), F32),
                        pltpu.VMEM((bsz, nr, POOL_WIDTH), F32),
                        pltpu.VMEM((bsz, nr, POOL_WIDTH), F32),
                        pltpu.VMEM((bsz, nr, POOL_WIDTH - LANES), F32),
                        pltpu.VMEM((bsz, nr, POOL_WIDTH - 2 * LANES), F32)],
        compiler_params=_params(),
        name="mixer",
    )(h.reshape(bsz, seqlen, D_MODEL), *consts)

    consts = [w1b, w3b, w2b, wgb, wpb, row(ple_norm[i]), row(final_norm)]
    h = pl.pallas_call(
        _ffn2_kernel,
        grid=(n_tok // TM,),
        in_specs=[tok_spec, pl.BlockSpec((TM, PLE_DIM), lambda t: (t, 0))]
                 + [_const_spec(c.shape) for c in consts],
        out_specs=tok_spec,
        out_shape=jax.ShapeDtypeStruct((n_tok, D_MODEL), F32),
        scratch_shapes=[pltpu.VMEM((TM, D_FF), BF16)],
        compiler_params=_params(),
        name="ffn2",
    )(h.reshape(n_tok, D_MODEL), p[i].reshape(n_tok, PLE_DIM), *consts)
    return h.reshape(bsz, seqlen, D_MODEL)
```
